```python
import math, functools
import jax, jax.numpy as jnp
from jax import lax
import numpy as np

D_MODEL = 1024
BATCH = 4
SEQ = 4096
DEPTH = 4
DEC_BATCH = 128
DEC_SEQ = 8
PAST_LEN = 2048
PAGE_SIZE = 128

D_A = 512
CONV_W = 31
N_HEADS = 8
N_KV = 2
GROUP = N_HEADS // N_KV
HEAD_DIM = 64
D_ATT = N_HEADS * HEAD_DIM
N_IDX_HEADS = 4
D_IDX = 64
TOPK_MAX = 256
N_BUCKETS = 32
MAX_DIST = 128
Q_BLOCK = 128
D_C = 512
N_CGROUPS = 8
CHUNK = 128
N_EXPERTS = 16
N_GROUPS = 4
EXPERTS_PER_GROUP = N_EXPERTS // N_GROUPS
TOP_K_EXPERTS = 2
D_EXPERT = 512
ALPHA = (2 * DEPTH) ** 0.25
BETA = (8 * DEPTH) ** -0.25
LN_EPS = 1e-5

SPLIT_SIZES = [2 * D_A, D_ATT, N_KV * HEAD_DIM, N_KV * HEAD_DIM, N_IDX_HEADS * D_IDX, D_IDX, N_IDX_HEADS, 2 * D_C, 3 * D_MODEL]
SPLIT_OFFSETS = [int(o) for o in np.cumsum(SPLIT_SIZES)[:-1]]
D_IN = int(sum(SPLIT_SIZES))

kernel_name = "hybrid_conv_dsa_gmlp_moe_step"


def layer_norm(x, g, b):
    xf = x.astype(jnp.float32)
    mu = jnp.mean(xf, axis=-1, keepdims=True)
    var = jnp.mean(jnp.square(xf - mu), axis=-1, keepdims=True)
    return ((xf - mu) * lax.rsqrt(var + LN_EPS)).astype(x.dtype) * g + b


def t5_bucket(rel):
    n = jnp.maximum(-rel, 0)
    max_exact = N_BUCKETS // 2
    nf = jnp.maximum(n, 1).astype(jnp.float32)
    large = max_exact + (jnp.log(nf / max_exact) / math.log(MAX_DIST / max_exact) * (N_BUCKETS - max_exact)).astype(jnp.int32)
    large = jnp.minimum(large, N_BUCKETS - 1)
    return jnp.where(n < max_exact, n, large)


def conformer_conv(a_glu, conv_buf, conv_k, conv_b, ln_g, ln_b):
    a, a_gate = jnp.split(a_glu, 2, axis=-1)
    a = a * jax.nn.sigmoid(a_gate)
    ext = jnp.concatenate([conv_buf.astype(a.dtype), a], axis=1)
    y = lax.conv_general_dilated(ext, conv_k[:, None, :].astype(a.dtype), window_strides=(1,), padding='VALID',
                                 dimension_numbers=('NWC', 'WIO', 'NWC'), feature_group_count=D_A) + conv_b
    y = jax.nn.silu(layer_norm(y, ln_g, ln_b))
    return y, ext[:, -(CONV_W - 1):]


def select_and_attend(qb, qib, wib, qpos, ki_all, kpos, gather_kv, rel_table, topk):
    B, Q = qb.shape[:2]
    qib = qib.reshape(B, Q, N_IDX_HEADS, D_IDX)
    dots = jnp.einsum('bqhd,bsd->bqhs', qib, ki_all).astype(jnp.float32)
    score = jnp.einsum('bqh,bqhs->bqs', wib.astype(jnp.float32), jax.nn.relu(dots))
    admissible = kpos[None, None, :] <= qpos[None, :, None]
    score = jnp.where(admissible, score, -jnp.inf)
    _, idx = lax.top_k(score, topk)
    valid = idx <= qpos[None, :, None]
    ks, vs = gather_kv(idx)
    qg = qb.reshape(B, Q, N_KV, GROUP, HEAD_DIM)
    logits = jnp.einsum('bqngd,bqknd->bqngk', qg, ks).astype(jnp.float32) * (HEAD_DIM ** -0.5)
    bias = rel_table[t5_bucket(idx - qpos[None, :, None])]
    bias = bias.reshape(B, Q, topk, N_KV, GROUP).transpose(0, 1, 3, 4, 2).astype(jnp.float32)
    logits = jnp.where(valid[:, :, None, None, :], logits + bias, -jnp.inf)
    p = jax.nn.softmax(logits, axis=-1).astype(vs.dtype)
    o = jnp.einsum('bqngk,bqknd->bqngd', p, vs)
    return o.reshape(B, Q, D_ATT)


def dsa_prompt(q, k, v, qi, ki, wi, rel_table):
    B, T = q.shape[:2]
    topk = min(TOPK_MAX, T // 4)
    kpos = jnp.arange(T)

    def gather_kv(idx):
        take = jax.vmap(lambda rows, i: rows[i])
        return take(k, idx), take(v, idx)

    def block(start):
        qb = lax.dynamic_slice_in_dim(q, start, Q_BLOCK, axis=1)
        qib = lax.dynamic_slice_in_dim(qi, start, Q_BLOCK, axis=1)
        wib = lax.dynamic_slice_in_dim(wi, start, Q_BLOCK, axis=1)
        qpos = start + jnp.arange(Q_BLOCK)
        return select_and_attend(qb, qib, wib, qpos, ki, kpos, gather_kv, rel_table, topk)

    out = lax.map(block, jnp.arange(T // Q_BLOCK) * Q_BLOCK)
    return jnp.moveaxis(out, 0, 1).reshape(B, T, D_ATT)


def dsa_sample(q, k, v, qi, ki, wi, cache_k_l, cache_v_l, cache_ki_l, page_table, rel_table):
    B, T = q.shape[:2]
    P = page_table.shape[1] * PAGE_SIZE
    L = P + T
    topk = min(TOPK_MAX, L // 4)
    ki_past = cache_ki_l[page_table].reshape(B, P, D_IDX)
    ki_all = jnp.concatenate([ki_past.astype(ki.dtype), ki], axis=1)
    kpos = jnp.arange(L)
    qpos = P + jnp.arange(T)
    take = jax.vmap(lambda rows, i: rows[i])

    def gather_kv(idx):
        in_past = (idx < P)[..., None, None]
        pidx = jnp.minimum(idx, P - 1)
        phys = take(page_table, pidx // PAGE_SIZE)
        off = pidx % PAGE_SIZE
        nidx = jnp.clip(idx - P, 0, T - 1)
        ks = jnp.where(in_past, cache_k_l[phys, off].astype(k.dtype), take(k, nidx))
        vs = jnp.where(in_past, cache_v_l[phys, off].astype(v.dtype), take(v, nidx))
        return ks, vs

    return select_and_attend(q, qi, wi, qpos, ki_all, kpos, gather_kv, rel_table, topk)


def chunk_sgu(uv, ln_g, ln_b, w_s, b_s, rows):
    B, T = uv.shape[:2]
    u, vv = jnp.split(uv, 2, axis=-1)
    vv = layer_norm(vv, ln_g, ln_b)
    mask = jnp.tril(jnp.ones((rows, rows), dtype=bool))
    ws = jnp.where(mask, w_s[:, :rows, :rows], 0).astype(vv.dtype)
    vc = vv.reshape(B, T // rows, rows, N_CGROUPS, D_C // N_CGROUPS)
    mixed = jnp.einsum('gts,bcsgd->bctgd', ws, vc) + b_s[:, :rows].T[None, None, :, :, None]
    return u * mixed.reshape(B, T, D_C), vv


def moe(h, w_router, router_bias, w_e_gate, w_e_up, w_e_down):
    B, T, D = h.shape
    scores = jax.nn.sigmoid(jnp.einsum('btd,de->bte', h, w_router).astype(jnp.float32))
    sel = scores + router_bias.astype(jnp.float32)
    grp_score = lax.top_k(sel.reshape(B, T, N_GROUPS, EXPERTS_PER_GROUP), 2)[0].sum(-1)
    best = jnp.argmax(grp_score, axis=-1)
    in_group = (jnp.arange(N_EXPERTS)[None, None, :] // EXPERTS_PER_GROUP) == best[..., None]
    _, top_idx = lax.top_k(jnp.where(in_group, sel, -jnp.inf), TOP_K_EXPERTS)
    top_w = jnp.take_along_axis(scores, top_idx, axis=-1)
    top_w = top_w / jnp.sum(top_w, axis=-1, keepdims=True)
    gate = jnp.sum(jax.nn.one_hot(top_idx, N_EXPERTS, dtype=jnp.float32) * top_w[..., None], axis=-2).astype(h.dtype)
    y = jnp.zeros_like(h)
    for e in range(N_EXPERTS):
        he = jax.nn.silu(h @ w_e_gate[e]) * (h @ w_e_up[e])
        y = y + gate[..., e:e + 1] * (he @ w_e_down[e])
    return y


def trunk_layer(x, c, conv_buf, attend, chunk_rows, w_ada, b_ada, w_in, b_in, conv_k, conv_b, lnA_g, lnA_b,
                lnC_g, lnC_b, w_s, b_s, w_pa, w_pb, w_pc, w_o, ln1_g, ln1_b, ln2_g, ln2_b,
                w_router, router_bias, w_e_gate, w_e_up, w_e_down):
    B, T = x.shape[:2]
    mod = jnp.einsum('bd,de->be', jax.nn.silu(c), w_ada) + b_ada
    sh1, sc1, g1, sh2, sc2, g2 = [m[:, None, :] for m in jnp.split(mod, 6, axis=-1)]
    h = x * (1 + sc1) + sh1
    z = jnp.einsum('btd,de->bte', h, w_in) + b_in
    a_glu, q, k, v, qi, ki, wi, uv, gate_logits = jnp.split(z, SPLIT_OFFSETS, axis=-1)
    y_a, conv_state = conformer_conv(a_glu, conv_buf, conv_k, conv_b, lnA_g, lnA_b)
    k = k.reshape(B, T, N_KV, HEAD_DIM)
    v = v.reshape(B, T, N_KV, HEAD_DIM)
    y_b = attend(q, k, v, qi, ki, wi)
    y_c, v_chunk = chunk_sgu(uv, lnC_g, lnC_b, w_s, b_s, chunk_rows)
    g_a, g_b, g_c = jnp.split(jax.nn.sigmoid(gate_logits), 3, axis=-1)
    merged = g_a * (y_a @ w_pa) + g_b * (y_b @ w_pb) + g_c * (y_c @ w_pc)
    x = layer_norm(ALPHA * x + (1 + g1) * (merged @ w_o), ln1_g, ln1_b)
    h2 = x * (1 + sc2) + sh2
    x = layer_norm(ALPHA * x + (1 + g2) * moe(h2, w_router, router_bias, w_e_gate, w_e_up, w_e_down), ln2_g, ln2_b)
    return x, k, v, ki, conv_state, v_chunk


def setup_inputs(seed: int = 0) -> dict:
    key = jax.random.key(seed)
    ks = jax.random.split(key, 40)
    f = jnp.float32

    def nrm(k, shape, scale):
        return jax.random.normal(k, shape, f) * scale

    n_pages = PAST_LEN // PAGE_SIZE
    n_used = DEC_BATCH * n_pages
    n_pool = n_used + max(1, n_used // 4)
    page_table = jax.random.permutation(ks[0], n_pool)[:n_used].reshape(DEC_BATCH, n_pages).astype(jnp.int32)
    return {
        "x_prompt": nrm(ks[1], (BATCH, SEQ, D_MODEL), 1.0),
        "x_sample": nrm(ks[2], (DEC_BATCH, DEC_SEQ, D_MODEL), 1.0),
        "cache_k": nrm(ks[3], (DEPTH, n_pool, PAGE_SIZE, N_KV, HEAD_DIM), 1.0),
        "cache_v": nrm(ks[4], (DEPTH, n_pool, PAGE_SIZE, N_KV, HEAD_DIM), 1.0),
        "cache_kidx": nrm(ks[5], (DEPTH, n_pool, PAGE_SIZE, D_IDX), 1.0),
        "state_conv": nrm(ks[6], (DEPTH, DEC_BATCH, CONV_W - 1, D_A), 0.5),
        "page_table": page_table,
        "c_prompt": nrm(ks[7], (BATCH, D_MODEL), 1.0),
        "c_sample": nrm(ks[8], (DEC_BATCH, D_MODEL), 1.0),
        "w_ada": nrm(ks[9], (DEPTH, D_MODEL, 6 * D_MODEL), 0.1 * D_MODEL ** -0.5),
        "b_ada": nrm(ks[10], (DEPTH, 6 * D_MODEL), 0.02),
        "w_in": nrm(ks[11], (DEPTH, D_MODEL, D_IN), D_MODEL ** -0.5),
        "b_in": nrm(ks[12], (DEPTH, D_IN), 0.02),
        "conv_k": nrm(ks[13], (DEPTH, CONV_W, D_A), CONV_W ** -0.5),
        "conv_b": nrm(ks[14], (DEPTH, D_A), 0.02),
        "lnA_g": 1.0 + nrm(ks[15], (DEPTH, D_A), 0.05),
        "lnA_b": nrm(ks[16], (DEPTH, D_A), 0.02),
        "lnC_g": 1.0 + nrm(ks[17], (DEPTH, D_C), 0.05),
        "lnC_b": nrm(ks[18], (DEPTH, D_C), 0.02),
        "w_s": nrm(ks[19], (DEPTH, N_CGROUPS, CHUNK, CHUNK), CHUNK ** -0.5),
        "b_s": 1.0 + nrm(ks[20], (DEPTH, N_CGROUPS, CHUNK), 0.05),
        "w_pa": nrm(ks[21], (DEPTH, D_A, D_MODEL), BETA * D_A ** -0.5),
        "w_pb": nrm(ks[22], (DEPTH, D_ATT, D_MODEL), BETA * D_ATT ** -0.5),
        "w_pc": nrm(ks[23], (DEPTH, D_C, D_MODEL), BETA * D_C ** -0.5),
        "w_o": nrm(ks[24], (DEPTH, D_MODEL, D_MODEL), BETA * D_MODEL ** -0.5),
        "ln1_g": 1.0 + nrm(ks[25], (DEPTH, D_MODEL), 0.05),
        "ln1_b": nrm(ks[26], (DEPTH, D_MODEL), 0.02),
        "ln2_g": 1.0 + nrm(ks[27], (DEPTH, D_MODEL), 0.05),
        "ln2_b": nrm(ks[28], (DEPTH, D_MODEL), 0.02),
        "rel_table": nrm(ks[29], (N_BUCKETS, N_HEADS), 0.5),
        "w_router": nrm(ks[30], (D_MODEL, N_EXPERTS), D_MODEL ** -0.5),
        "router_bias": nrm(ks[31], (N_EXPERTS,), 0.01),
        "w_e_gate": nrm(ks[32], (DEPTH, N_EXPERTS, D_MODEL, D_EXPERT), D_MODEL ** -0.5),
        "w_e_up": nrm(ks[33], (DEPTH, N_EXPERTS, D_MODEL, D_EXPERT), D_MODEL ** -0.5),
        "w_e_down": nrm(ks[34], (DEPTH, N_EXPERTS, D_EXPERT, D_MODEL), BETA * D_EXPERT ** -0.5),
    }


def reference(x_prompt, x_sample, cache_k, cache_v, cache_kidx, state_conv, page_table, c_prompt, c_sample,
              w_ada, b_ada, w_in, b_in, conv_k, conv_b, lnA_g, lnA_b, lnC_g, lnC_b, w_s, b_s,
              w_pa, w_pb, w_pc, w_o, ln1_g, ln1_b, ln2_g, ln2_b, rel_table, w_router, router_bias,
              w_e_gate, w_e_up, w_e_down):
    yp, ys = x_prompt, x_sample
    kp, vp, kip, cp = [], [], [], []
    ks_, vs_, kis, cs, vcs = [], [], [], [], []
    attend_p = functools.partial(dsa_prompt, rel_table=rel_table)
    for l in range(DEPTH):
        lp = (w_ada[l], b_ada[l], w_in[l], b_in[l], conv_k[l], conv_b[l], lnA_g[l], lnA_b[l], lnC_g[l], lnC_b[l],
              w_s[l], b_s[l], w_pa[l], w_pb[l], w_pc[l], w_o[l], ln1_g[l], ln1_b[l], ln2_g[l], ln2_b[l],
              w_router, router_bias, w_e_gate[l], w_e_up[l], w_e_down[l])
        buf0 = jnp.zeros((yp.shape[0], CONV_W - 1, D_A), yp.dtype)
        yp, k_l, v_l, ki_l, conv_l, _ = trunk_layer(yp, c_prompt, buf0, attend_p, CHUNK, *lp)
        kp.append(k_l); vp.append(v_l); kip.append(ki_l); cp.append(conv_l)
        attend_s = functools.partial(dsa_sample, cache_k_l=cache_k[l], cache_v_l=cache_v[l], cache_ki_l=cache_kidx[l],
                                     page_table=page_table, rel_table=rel_table)
        ys, k_s, v_s, ki_s, conv_s, vc_s = trunk_layer(ys, c_sample, state_conv[l], attend_s, ys.shape[1], *lp)
        ks_.append(k_s); vs_.append(v_s); kis.append(ki_s); cs.append(conv_s); vcs.append(vc_s)
    new_k_prompt = jnp.stack(kp)
    new_v_prompt = jnp.stack(vp)
    new_kidx_prompt = jnp.stack(kip)
    new_conv_prompt = jnp.stack(cp)
    new_k_sample = jnp.stack(ks_)
    new_v_sample = jnp.stack(vs_)
    new_kidx_sample = jnp.stack(kis)
    new_conv_sample = jnp.stack(cs)
    new_vchunk_sample = jnp.stack(vcs)
    return (yp, ys, new_k_prompt, new_v_prompt, new_kidx_prompt, new_conv_prompt,
            new_k_sample, new_v_sample, new_kidx_sample, new_conv_sample, new_vchunk_sample)
```

```python
import functools
import math

import jax
import jax.numpy as jnp
from jax import lax
from jax.experimental import pallas as pl
from jax.experimental.pallas import tpu as pltpu

F32 = jnp.float32
BF16 = jnp.bfloat16
I32 = jnp.int32

D_MODEL = 1024
D_A = 512
CONV_W = 31
N_HEADS = 8
N_KV = 2
GROUP = N_HEADS // N_KV
HEAD_DIM = 64
D_ATT = N_HEADS * HEAD_DIM
N_IDX_HEADS = 4
D_IDX = 64
TOPK_MAX = 256
N_BUCKETS = 32
MAX_DIST = 128
D_C = 512
N_CGROUPS = 8
CHUNK = 128
N_EXPERTS = 16
N_GROUPS = 4
EXPERTS_PER_GROUP = N_EXPERTS // N_GROUPS
D_EXPERT = 512
PAGE_SIZE = 128
LN_EPS = 1e-5

LANES = 128
KEY_BLK = 128
FAR_BLKS = 4
NEG_BIG = -1e30
INT_MIN = -2 ** 31
VMEM_LIMIT = 56 * 1024 * 1024

C_A, C_AG, C_U, C_V = 0, 512, 1024, 1536
C_GA, C_GB, C_GC = 2048, 3072, 4096
C_Q, C_QI, C_K, C_VV, C_KI4, C_WI = 5120, 5632, 5888, 6016, 6144, 6400
D_ZP = 6656

_O_AGLU, _O_Q, _O_K, _O_V, _O_QI, _O_KI, _O_WI, _O_UV, _O_G = 0, 1024, 1536, 1664, 1792, 2048, 2112, 2116, 3140
_D_IN = 6212


def _cp(n_axes):
    return pltpu.CompilerParams(dimension_semantics=("arbitrary",) * n_axes, vmem_limit_bytes=VMEM_LIMIT)


def _sigmoid(x):
    return jax.nn.sigmoid(x)


def _ln(x, g, b):
    mu = jnp.mean(x, axis=-1, keepdims=True)
    xc = x - mu
    var = jnp.mean(xc * xc, axis=-1, keepdims=True)
    return xc * lax.rsqrt(var + LN_EPS) * g + b


def _dot(a, b):
    return jnp.dot(a, b, preferred_element_type=F32)


def _dot_nt(a, b):
    return lax.dot_general(a, b, (((1,), (1,)), ((), ())), preferred_element_type=F32)


def _ada_kernel(c_ref, w_ref, b_ref, o_ref):
    c = c_ref[...]
    s = (c * _sigmoid(c)).astype(BF16)
    o_ref[0] = _dot(s, w_ref[0].astype(BF16)) + b_ref[0]


def _ada(c_all, w_ada, b_ada):
    depth, d, n = w_ada.shape
    m = c_all.shape[0]
    tn = 1536
    return pl.pallas_call(
        _ada_kernel,
        grid=(depth, n // tn),
        in_specs=[pl.BlockSpec((m, d), lambda l, j: (0, 0)),
                  pl.BlockSpec((1, d, tn), lambda l, j: (l, 0, j)),
                  pl.BlockSpec((1, 1, tn), lambda l, j: (l, 0, j))],
        out_specs=pl.BlockSpec((1, m, tn), lambda l, j: (l, 0, j)),
        out_shape=jax.ShapeDtypeStruct((depth, m, n), F32),
        compiler_params=_cp(2),
        name="ada",
    )(c_all, w_ada, b_ada.reshape(depth, 1, n))


def _inproj_kernel(x_ref, sh_ref, sc_ref, w_ref, b_ref, z_ref):
    gb, rb, d = x_ref.shape
    h = x_ref[...] * (1.0 + sc_ref[...]) + sh_ref[...]
    z = _dot(h.reshape(gb * rb, d).astype(BF16), w_ref[...]) + b_ref[...]
    z_ref[...] = z.reshape(gb, rb, z.shape[-1])


def _inproj(x3, mod, w, b, gb, rb, tn):
    g, r, d = x3.shape
    n = w.shape[1]
    return pl.pallas_call(
        _inproj_kernel,
        grid=(n // tn, g // gb, r // rb),
        in_specs=[pl.BlockSpec((gb, rb, d), lambda j, gi, ri: (gi, ri, 0)),
                  pl.BlockSpec((gb, 1, d), lambda j, gi, ri: (gi, 0, 0)),
                  pl.BlockSpec((gb, 1, d), lambda j, gi, ri: (gi, 0, 1)),
                  pl.BlockSpec((d, tn), lambda j, gi, ri: (0, j)),
                  pl.BlockSpec((1, tn), lambda j, gi, ri: (0, j))],
        out_specs=pl.BlockSpec((gb, rb, tn), lambda j, gi, ri: (gi, ri, j)),
        out_shape=jax.ShapeDtypeStruct((g, r, n), F32),
        compiler_params=_cp(3),
        name="inproj",
    )(x3, mod, mod, w, b)


CONV_HALO = 32
CONV_ROWS = 32


def _conv_prompt_kernel(a_ref, g_ref, ah_ref, gh_ref, ck_ref, cb_ref, lg_ref, lb_ref, y_ref, st_ref, ext_scr):
    t = pl.program_id(1)
    tt = a_ref.shape[1]
    a = a_ref[0] * _sigmoid(g_ref[0])
    halo = ah_ref[0] * _sigmoid(gh_ref[0])
    ext_scr[0:CONV_HALO, :] = jnp.where(t > 0, halo, 0.0)
    ext_scr[CONV_HALO:, :] = a
    off = CONV_HALO - (CONV_W - 1)
    for c in range(tt // CONV_ROWS):
        r0 = c * CONV_ROWS
        acc = jnp.zeros((CONV_ROWS, D_A), F32) + cb_ref[...]
        for j in range(CONV_W):
            acc = acc + ck_ref[j:j + 1, :] * ext_scr[r0 + off + j:r0 + off + j + CONV_ROWS, :]
        y = _ln(acc, lg_ref[...], lb_ref[...])
        y_ref[0, r0:r0 + CONV_ROWS, :] = (y * _sigmoid(y)).astype(y_ref.dtype)

    @pl.when(t == pl.num_programs(1) - 1)
    def _():
        st_ref[0] = ext_scr[tt + off:tt + CONV_HALO, :]


def _conv_prompt(z, ck, cb, lg, lb, tt=256):
    b, t, _ = z.shape
    hb = tt // CONV_HALO
    return pl.pallas_call(
        _conv_prompt_kernel,
        grid=(b, t // tt),
        in_specs=[pl.BlockSpec((1, tt, D_A), lambda bi, ti: (bi, ti, C_A // D_A)),
                  pl.BlockSpec((1, tt, D_A), lambda bi, ti: (bi, ti, C_AG // D_A)),
                  pl.BlockSpec((1, CONV_HALO, D_A), lambda bi, ti: (bi, jnp.maximum(ti * hb - 1, 0), C_A // D_A)),
                  pl.BlockSpec((1, CONV_HALO, D_A), lambda bi, ti: (bi, jnp.maximum(ti * hb - 1, 0), C_AG // D_A)),
                  pl.BlockSpec((CONV_W, D_A), lambda bi, ti: (0, 0)),
                  pl.BlockSpec((1, D_A), lambda bi, ti: (0, 0)),
                  pl.BlockSpec((1, D_A), lambda bi, ti: (0, 0)),
                  pl.BlockSpec((1, D_A), lambda bi, ti: (0, 0))],
        out_specs=[pl.BlockSpec((1, tt, D_A), lambda bi, ti: (bi, ti, 0)),
                   pl.BlockSpec((1, CONV_W - 1, D_A), lambda bi, ti: (bi, 0, 0))],
        out_shape=[jax.ShapeDtypeStruct((b, t, D_A), BF16),
                   jax.ShapeDtypeStruct((b, CONV_W - 1, D_A), F32)],
        scratch_shapes=[pltpu.VMEM((tt + CONV_HALO, D_A), F32)],
        compiler_params=_cp(2),
        name="conv_prompt",
    )(z, z, z, z, ck, cb, lg, lb)


def _conv_sample_kernel(a_ref, g_ref, st_ref, ck_ref, cb_ref, lg_ref, lb_ref, y_ref, ns_ref, ext_scr):
    bs, t, _ = a_ref.shape
    hist = CONV_W - 1
    ext_scr[:, 0:hist, :] = st_ref[...]
    ext_scr[:, hist:hist + t, :] = a_ref[...] * _sigmoid(g_ref[...])
    ns_ref[...] = ext_scr[:, t:t + hist, :]

    def body(b, carry):
        acc = jnp.zeros((t, D_A), F32) + cb_ref[...]
        for j in range(CONV_W):
            acc = acc + ck_ref[j:j + 1, :] * ext_scr[b, j:j + t, :]
        y = _ln(acc, lg_ref[...], lb_ref[...])
        y_ref[b] = y * _sigmoid(y)
        return carry

    lax.fori_loop(0, bs, body, 0)


def _conv_sample(z, state, ck, cb, lg, lb, bs=32):
    b, t, _ = z.shape
    hist = CONV_W - 1
    bs = min(bs, b)
    return pl.pallas_call(
        _conv_sample_kernel,
        grid=(b // bs,),
        in_specs=[pl.BlockSpec((bs, t, D_A), lambda bi: (bi, 0, C_A // D_A)),
                  pl.BlockSpec((bs, t, D_A), lambda bi: (bi, 0, C_AG // D_A)),
                  pl.BlockSpec((bs, hist, D_A), lambda bi: (bi, 0, 0)),
                  pl.BlockSpec((CONV_W, D_A), lambda bi: (0, 0)),
                  pl.BlockSpec((1, D_A), lambda bi: (0, 0)),
                  pl.BlockSpec((1, D_A), lambda bi: (0, 0)),
                  pl.BlockSpec((1, D_A), lambda bi: (0, 0))],
        out_specs=[pl.BlockSpec((bs, t, D_A), lambda bi: (bi, 0, 0)),
                   pl.BlockSpec((bs, hist, D_A), lambda bi: (bi, 0, 0))],
        out_shape=[jax.ShapeDtypeStruct((b, t, D_A), F32),
                   jax.ShapeDtypeStruct((b, hist, D_A), F32)],
        scratch_shapes=[pltpu.VMEM((bs, hist + t + 2, D_A), F32)],
        compiler_params=_cp(1),
        name="conv_sample",
    )(z, z, state, ck, cb, lg, lb)


def _sgu_prompt_kernel(u_ref, v_ref, lg_ref, lb_ref, ws_ref, bs_ref, y_ref):
    tt = u_ref.shape[1]
    vv = _ln(v_ref[0], lg_ref[...], lb_ref[...])
    row = lax.broadcasted_iota(I32, (CHUNK, CHUNK), 0)
    col = lax.broadcasted_iota(I32, (CHUNK, CHUNK), 1)
    lane_grp = lax.broadcasted_iota(I32, (CHUNK, D_C), 1) // (D_C // N_CGROUPS)
    ws = [jnp.where(col <= row, ws_ref[g], 0.0).astype(BF16) for g in range(N_CGROUPS)]
    for c in range(tt // CHUNK):
        vc = vv[c * CHUNK:(c + 1) * CHUNK].astype(BF16)
        mixed = bs_ref[...]
        for g in range(N_CGROUPS):
            mixed = mixed + _dot(ws[g], jnp.where(lane_grp == g, vc, jnp.zeros_like(vc)))
        y_ref[0, c * CHUNK:(c + 1) * CHUNK, :] = (u_ref[0, c * CHUNK:(c + 1) * CHUNK, :] * mixed).astype(y_ref.dtype)


def _sgu_prompt(z, lg, lb, w_s, bs_tab, tt=512):
    b, t, _ = z.shape
    return pl.pallas_call(
        _sgu_prompt_kernel,
        grid=(b, t // tt),
        in_specs=[pl.BlockSpec((1, tt, D_C), lambda bi, ti: (bi, ti, C_U // D_C)),
                  pl.BlockSpec((1, tt, D_C), lambda bi, ti: (bi, ti, C_V // D_C)),
                  pl.BlockSpec((1, D_C), lambda bi, ti: (0, 0)),
                  pl.BlockSpec((1, D_C), lambda bi, ti: (0, 0)),
                  pl.BlockSpec((N_CGROUPS, CHUNK, CHUNK), lambda bi, ti: (0, 0, 0)),
                  pl.BlockSpec((CHUNK, D_C), lambda bi, ti: (0, 0))],
        out_specs=pl.BlockSpec((1, tt, D_C), lambda bi, ti: (bi, ti, 0)),
        out_shape=jax.ShapeDtypeStruct((b, t, D_C), BF16),
        compiler_params=_cp(2),
        name="sgu_prompt",
    )(z, z, lg, lb, w_s, bs_tab)


def _sgu_sample_kernel(u_ref, v_ref, lg_ref, lb_ref, wl_ref, bs_ref, y_ref, vc_ref):
    bs, t, _ = u_ref.shape
    vv = _ln(v_ref[...], lg_ref[...], lb_ref[...])
    vc_ref[...] = vv
    tpos = lax.broadcasted_iota(I32, (t, D_C), 0)
    mixed = jnp.zeros((bs, t, D_C), F32) + bs_ref[...]
    for s in range(t):
        w = jnp.where(tpos >= s, wl_ref[s], 0.0)
        mixed = mixed + w * vv[:, s:s + 1, :]
    y_ref[...] = u_ref[...] * mixed


def _sgu_sample(z, lg, lb, wl_tab, bs_tab):
    b, t, _ = z.shape
    return pl.pallas_call(
        _sgu_sample_kernel,
        grid=(1,),
        in_specs=[pl.BlockSpec((b, t, D_C), lambda i: (0, 0, C_U // D_C)),
                  pl.BlockSpec((b, t, D_C), lambda i: (0, 0, C_V // D_C)),
                  pl.BlockSpec((1, D_C), lambda i: (0, 0)),
                  pl.BlockSpec((1, D_C), lambda i: (0, 0)),
                  pl.BlockSpec((t, t, D_C), lambda i: (0, 0, 0)),
                  pl.BlockSpec((t, D_C), lambda i: (0, 0))],
        out_specs=[pl.BlockSpec((b, t, D_C), lambda i: (0, 0, 0)),
                   pl.BlockSpec((b, t, D_C), lambda i: (0, 0, 0))],
        out_shape=[jax.ShapeDtypeStruct((b, t, D_C), F32),
                   jax.ShapeDtypeStruct((b, t, D_C), F32)],
        compiler_params=_cp(1),
        name="sgu_sample",
    )(z, z, lg, lb, wl_tab, bs_tab)


def _bias_kernel(tab_ref, o_ref):
    rows, cols = KEY_BLK, 2 * KEY_BLK
    i = lax.broadcasted_iota(I32, (rows, cols), 0)
    j = lax.broadcasted_iota(I32, (rows, cols), 1)
    n = jnp.maximum(KEY_BLK + i - j, 0)
    max_exact = N_BUCKETS // 2
    nf = jnp.maximum(n, 1).astype(F32)
    large = max_exact + (jnp.log(nf / max_exact) / math.log(MAX_DIST / max_exact) * (N_BUCKETS - max_exact)).astype(I32)
    large = jnp.minimum(large, N_BUCKETS - 1)
    bucket = jnp.where(n < max_exact, n, large)
    for h in range(N_HEADS):
        acc = jnp.zeros((rows, cols), F32)
        for bkt in range(N_BUCKETS):
            acc = jnp.where(bucket == bkt, tab_ref[bkt, h], acc)
        o_ref[h // GROUP, (h % GROUP) * rows:(h % GROUP + 1) * rows, :] = acc


def _bias_near(rel_table):
    return pl.pallas_call(
        _bias_kernel,
        grid=(1,),
        in_specs=[pl.BlockSpec(memory_space=pltpu.SMEM)],
        out_specs=pl.BlockSpec((N_KV, GROUP * KEY_BLK, 2 * KEY_BLK), lambda i: (0, 0, 0)),
        out_shape=jax.ShapeDtypeStruct((N_KV, GROUP * KEY_BLK, 2 * KEY_BLK), F32),
        compiler_params=_cp(1),
        name="bias_near",
    )(rel_table)


def _score_keys(s):
    s = jnp.where(s == 0.0, 0.0, s)
    bits = lax.bitcast_convert_type(s, I32)
    return bits ^ (jnp.right_shift(bits, 31) & 0x7FFFFFFF)


def _topk_threshold(load_blk, n_iter, unroll, rows, k, idx_bits, cut_scr):
    lane = lax.broadcasted_iota(I32, (rows, KEY_BLK), 1)

    def count(pred):
        def body(jj, acc):
            for u in range(unroll):
                j = jj * unroll + u
                acc = acc + jnp.where(pred(load_blk(j), j), 1.0, 0.0)
            return acc
        acc = lax.fori_loop(0, n_iter, body, jnp.zeros((rows, KEY_BLK), F32))
        return jnp.sum(acc, axis=1, keepdims=True)

    kf = float(k)

    def bit_body(i, thr):
        cand = thr ^ jnp.left_shift(jnp.int32(1), 31 - i)
        cnt = count(lambda kb, j: kb >= cand)
        return jnp.where(cnt >= kf, cand, thr)

    thr = lax.fori_loop(0, 32, bit_body, jnp.full((rows, 1), INT_MIN, I32))
    cnt_ge = count(lambda kb, j: kb >= thr)
    cnt_gt = count(lambda kb, j: kb > thr)
    need = kf - cnt_gt
    cut_scr[...] = jnp.full((rows, 1), 2 ** idx_bits, I32)

    @pl.when(jnp.max(cnt_ge) > kf)
    def _():
        def idx_body(i, r):
            cand = r + jnp.left_shift(jnp.int32(1), idx_bits - 1 - i)
            cnt = count(lambda kb, j: (kb == thr) & (j * KEY_BLK + lane < cand))
            return jnp.where(cnt < need, cand, r)
        r = lax.fori_loop(0, idx_bits, idx_body, jnp.zeros((rows, 1), I32))
        cut_scr[...] = r + 1

    return thr, cut_scr[...]


def _select_mask(keys, thr, cut, gidx):
    return (keys > thr) | ((keys == thr) & (gidx < cut))


def _stack_heads(q, n, rows):
    half = lax.broadcasted_iota(I32, (rows, LANES), 1) // HEAD_DIM
    tiles = []
    for g in range(GROUP):
        h = n * GROUP + g
        col = q[:, (h // 2) * LANES:(h // 2 + 1) * LANES]
        if h % 2 != n:
            col = pltpu.roll(col, HEAD_DIM, 1)
        tiles.append(jnp.where(half == n, col, 0.0))
    return jnp.concatenate(tiles, axis=0)


def _unstack_heads(o_groups, rows):
    half = lax.broadcasted_iota(I32, (rows, LANES), 1) // HEAD_DIM
    cols = []
    for c in range(N_HEADS // 2):
        parts = []
        for h in (2 * c, 2 * c + 1):
            n, g = h // GROUP, h % GROUP
            tile = o_groups[n][g * rows:(g + 1) * rows, :]
            if h % 2 != n:
                tile = pltpu.roll(tile, HEAD_DIM, 1)
            parts.append(tile)
        cols.append(jnp.where(half == 0, parts[0], parts[1]))
    return jnp.concatenate(cols, axis=1)


def _dsa_prompt_kernel(q_ref, qi_ref, wi_ref, ki_ref, k_ref, v_ref, bias_ref, cfar_ref, o_ref,
                       key_scr, cut_scr, qs_scr, m_scr, l_scr, acc_scr, *, topk):
    qb = pl.program_id(1)
    rows = KEY_BLK
    chunk = FAR_BLKS * KEY_BLK
    row_i = lax.broadcasted_iota(I32, (rows, 1), 0)
    tpos = qb * rows + row_i

    qi = qi_ref[0]
    lane_head = lax.broadcasted_iota(I32, qi.shape, 1) // D_IDX
    qim = [jnp.where(lane_head == h, qi, 0.0).astype(BF16) for h in range(N_IDX_HEADS)]
    wi = wi_ref[0]
    lane_c = lax.broadcasted_iota(I32, (rows, chunk), 1)
    n_chunks = qb // FAR_BLKS + 1

    def score_body(c, carry):
        kc = ki_ref[0, pl.ds(pl.multiple_of(c * chunk, chunk), chunk), :].astype(BF16)
        s = jnp.zeros((rows, chunk), F32)
        for h in range(N_IDX_HEADS):
            s = s + wi[:, h:h + 1] * jnp.maximum(_dot_nt(qim[h], kc), 0.0)
        s = jnp.where(c * chunk + lane_c <= tpos, s, -jnp.inf)
        keys = _score_keys(s)
        for j in range(FAR_BLKS):
            key_scr[c * FAR_BLKS + j] = keys[:, j * KEY_BLK:(j + 1) * KEY_BLK]
        return carry

    lax.fori_loop(0, n_chunks, score_body, 0)

    thr, cut = _topk_threshold(lambda j: key_scr[j], n_chunks, FAR_BLKS, rows, topk, 13, cut_scr)

    q = q_ref[0] * (HEAD_DIM ** -0.5)
    for n in range(N_KV):
        qs_scr[n] = _stack_heads(q, n, rows).astype(BF16)
    m_scr[...] = jnp.full(m_scr.shape, NEG_BIG, F32)
    l_scr[...] = jnp.zeros(l_scr.shape, F32)
    acc_scr[...] = jnp.zeros(acc_scr.shape, F32)

    def attend(kc, vc, neg_mask, bias_of):
        neg4 = jnp.concatenate([neg_mask] * GROUP, axis=0)
        for n in range(N_KV):
            lg = _dot_nt(qs_scr[n], kc) + bias_of(n) + neg4
            m_old = m_scr[n]
            m_new = jnp.maximum(m_old, jnp.max(lg, axis=1, keepdims=True))
            p = jnp.exp(lg - m_new)
            alpha = jnp.exp(m_old - m_new)
            l_scr[n] = alpha * l_scr[n] + jnp.sum(p, axis=1, keepdims=True)
            acc_scr[n] = alpha * acc_scr[n] + _dot(p.astype(BF16), vc)
            m_scr[n] = m_new

    far_end = (qb - 1) * rows
    n_far = (qb + 2) // FAR_BLKS

    def far_body(c, carry):
        start = pl.multiple_of(c * chunk, chunk)
        kc = k_ref[0, pl.ds(start, chunk), :].astype(BF16)
        vc = v_ref[0, pl.ds(start, chunk), :].astype(BF16)
        keys = jnp.concatenate([key_scr[c * FAR_BLKS + j] for j in range(FAR_BLKS)], axis=1)
        gidx = c * chunk + lane_c
        sel = _select_mask(keys, thr, cut, gidx) & (gidx < far_end)
        attend(kc, vc, jnp.where(sel, 0.0, NEG_BIG), lambda n: cfar_ref[n])
        return carry

    lax.fori_loop(0, n_far, far_body, 0)

    pb = jnp.maximum(qb - 1, 0)
    p0 = pl.multiple_of(pb * rows, rows)
    q0 = pl.multiple_of(qb * rows, rows)
    kc = jnp.concatenate([k_ref[0, pl.ds(p0, rows), :], k_ref[0, pl.ds(q0, rows), :]], axis=0).astype(BF16)
    vc = jnp.concatenate([v_ref[0, pl.ds(p0, rows), :], v_ref[0, pl.ds(q0, rows), :]], axis=0).astype(BF16)
    lane_b = lax.broadcasted_iota(I32, (rows, KEY_BLK), 1)
    g_prev = pb * rows + lane_b
    g_diag = qb * rows + lane_b
    sel_prev = _select_mask(key_scr[pb], thr, cut, g_prev) & (qb > 0)
    sel_diag = _select_mask(key_scr[qb], thr, cut, g_diag) & (g_diag <= tpos)
    neg = jnp.concatenate([jnp.where(sel_prev, 0.0, NEG_BIG), jnp.where(sel_diag, 0.0, NEG_BIG)], axis=1)
    attend(kc, vc, neg, lambda n: bias_ref[n])

    outs = [acc_scr[n] / l_scr[n] for n in range(N_KV)]
    o_ref[0] = _unstack_heads(outs, rows).astype(o_ref.dtype)


def _dsa_prompt(z, bias_near, cfar):
    b, t, _ = z.shape
    nb = t // KEY_BLK
    assert nb % FAR_BLKS == 0
    topk = min(TOPK_MAX, t // 4)
    grows = GROUP * KEY_BLK
    return pl.pallas_call(
        functools.partial(_dsa_prompt_kernel, topk=topk),
        grid=(b, nb),
        in_specs=[pl.BlockSpec((1, KEY_BLK, D_ATT), lambda bi, qi: (bi, qi, C_Q // D_ATT)),
                  pl.BlockSpec((1, KEY_BLK, 256), lambda bi, qi: (bi, qi, C_QI // 256)),
                  pl.BlockSpec((1, KEY_BLK, LANES), lambda bi, qi: (bi, qi, C_WI // LANES)),
                  pl.BlockSpec((1, t, 256), lambda bi, qi: (bi, 0, C_KI4 // 256)),
                  pl.BlockSpec((1, t, LANES), lambda bi, qi: (bi, 0, C_K // LANES)),
                  pl.BlockSpec((1, t, LANES), lambda bi, qi: (bi, 0, C_VV // LANES)),
                  pl.BlockSpec((N_KV, grows, 2 * KEY_BLK), lambda bi, qi: (0, 0, 0)),
                  pl.BlockSpec((N_KV, grows, 1), lambda bi, qi: (0, 0, 0))],
        out_specs=pl.BlockSpec((1, KEY_BLK, D_ATT), lambda bi, qi: (bi, qi, 0)),
        out_shape=jax.ShapeDtypeStruct((b, t, D_ATT), BF16),
        scratch_shapes=[pltpu.VMEM((nb, KEY_BLK, KEY_BLK), I32),
                        pltpu.VMEM((KEY_BLK, 1), I32),
                        pltpu.VMEM((N_KV, grows, LANES), BF16),
                        pltpu.VMEM((N_KV, grows, 1), F32),
                        pltpu.VMEM((N_KV, grows, 1), F32),
                        pltpu.VMEM((N_KV, grows, LANES), F32)],
        compiler_params=_cp(2),
        name="dsa_prompt",
    )(z, z, z, z, z, z, bias_near, cfar)


def _dsa_sample_score_kernel(pt_ref, qi_ref, wi_ref, kin_ref, *rest):
    n_pages = len(rest) - 1
    pages, o_ref = rest[:n_pages], rest[n_pages]
    t = qi_ref.shape[1]
    qi = qi_ref[0]
    wi = wi_ref[0]
    qs = jnp.concatenate([qi[:, h * D_IDX:(h + 1) * D_IDX] for h in range(N_IDX_HEADS)], axis=0).astype(BF16)
    k_past = jnp.concatenate([pg[0, 0] for pg in pages], axis=0).astype(BF16)
    k_new = jnp.concatenate([kin_ref[0][:, :D_IDX], jnp.zeros((KEY_BLK - t, D_IDX), F32)], axis=0).astype(BF16)

    def score(d):
        s = jnp.zeros((t, d.shape[1]), F32)
        for h in range(N_IDX_HEADS):
            s = s + wi[:, h:h + 1] * jnp.maximum(d[h * t:(h + 1) * t], 0.0)
        return s

    keys_past = _score_keys(score(_dot_nt(qs, k_past)))
    for j in range(n_pages):
        o_ref[j] = keys_past[:, j * KEY_BLK:(j + 1) * KEY_BLK]
    s_new = score(_dot_nt(qs, k_new))
    lane = lax.broadcasted_iota(I32, s_new.shape, 1)
    trow = lax.broadcasted_iota(I32, s_new.shape, 0)
    o_ref[n_pages] = _score_keys(jnp.where(lane <= trow, s_new, -jnp.inf))


def _page_specs(n_pages, layer, width):
    return [pl.BlockSpec((1, 1, PAGE_SIZE, width), lambda bi, pt, j=j: (layer, pt[bi, j], 0, 0))
            for j in range(n_pages)]


def _dsa_sample_scores(z, cache_kidx, page_table, layer):
    b, t, _ = z.shape
    n_pages = page_table.shape[1]
    nblk = n_pages + 1
    grid_spec = pltpu.PrefetchScalarGridSpec(
        num_scalar_prefetch=1,
        grid=(b,),
        in_specs=[pl.BlockSpec((1, t, 256), lambda bi, pt: (bi, 0, C_QI // 256)),
                  pl.BlockSpec((1, t, LANES), lambda bi, pt: (bi, 0, C_WI // LANES)),
                  pl.BlockSpec((1, t, 256), lambda bi, pt: (bi, 0, C_KI4 // 256))]
        + _page_specs(n_pages, layer, D_IDX),
        out_specs=pl.BlockSpec((nblk, t, KEY_BLK), lambda bi, pt: (0, bi, 0)),
    )
    return pl.pallas_call(
        _dsa_sample_score_kernel,
        grid_spec=grid_spec,
        out_shape=jax.ShapeDtypeStruct((nblk, b * t, KEY_BLK), I32),
        compiler_params=_cp(1),
        name="dsa_sample_scores",
    )(page_table, z, z, z, *([cache_kidx] * n_pages))


def _dsa_sample_topk_kernel(key_ref, thr_ref, cut_ref, cut_scr, *, topk):
    nblk, rows, _ = key_ref.shape
    thr, cut = _topk_threshold(lambda j: key_ref[j], 1, nblk, rows, topk, 13, cut_scr)
    thr_ref[...] = thr
    cut_ref[...] = cut


def _dsa_sample_topk(keys, topk, rows=128):
    nblk, m, _ = keys.shape
    return pl.pallas_call(
        functools.partial(_dsa_sample_topk_kernel, topk=topk),
        grid=(m // rows,),
        in_specs=[pl.BlockSpec((nblk, rows, KEY_BLK), lambda i: (0, i, 0))],
        out_specs=[pl.BlockSpec((rows, 1), lambda i: (i, 0)),
                   pl.BlockSpec((rows, 1), lambda i: (i, 0))],
        out_shape=[jax.ShapeDtypeStruct((m, 1), I32), jax.ShapeDtypeStruct((m, 1), I32)],
        scratch_shapes=[pltpu.VMEM((rows, 1), I32)],
        compiler_params=_cp(1),
        name="dsa_sample_topk",
    )(keys)


def _dsa_sample_attn_kernel(pt_ref, q_ref, kn_ref, vn_ref, key_ref, thr_ref, cut_ref, bias_ref, cfar_ref, *rest):
    n_pages = (len(rest) - 1) // 2
    kpages, vpages, o_ref = rest[:n_pages], rest[n_pages:2 * n_pages], rest[2 * n_pages]
    t = q_ref.shape[1]
    past = n_pages * PAGE_SIZE
    thr = thr_ref[...]
    cut = cut_ref[...]
    q = q_ref[0] * (HEAD_DIM ** -0.5)
    trow = lax.broadcasted_iota(I32, (t, 1), 0)

    n_far = n_pages - 1
    k_far = jnp.concatenate([pg[0, 0] for pg in kpages[:n_far]], axis=0).astype(BF16)
    v_far = jnp.concatenate([pg[0, 0] for pg in vpages[:n_far]], axis=0).astype(BF16)
    keys_far = jnp.concatenate([key_ref[j] for j in range(n_far)], axis=1)
    g_far = lax.broadcasted_iota(I32, keys_far.shape, 1)
    neg_far = jnp.where(_select_mask(keys_far, thr, cut, g_far), 0.0, NEG_BIG)

    pad = jnp.zeros((KEY_BLK - t, LANES), F32)
    k_near = jnp.concatenate([kpages[n_far][0, 0], kn_ref[0], pad], axis=0).astype(BF16)
    v_near = jnp.concatenate([vpages[n_far][0, 0], vn_ref[0], pad], axis=0).astype(BF16)
    lane_b = lax.broadcasted_iota(I32, (t, KEY_BLK), 1)
    sel_last = _select_mask(key_ref[n_far], thr, cut, n_far * KEY_BLK + lane_b)
    sel_new = _select_mask(key_ref[n_pages], thr, cut, past + lane_b) & (lane_b <= trow)
    neg_near = jnp.concatenate([jnp.where(sel_last, 0.0, NEG_BIG), jnp.where(sel_new, 0.0, NEG_BIG)], axis=1)

    outs = []
    for n in range(N_KV):
        qs = _stack_heads(q, n, t).astype(BF16)
        lg_far = _dot_nt(qs, k_far) + cfar_ref[n] + jnp.concatenate([neg_far] * GROUP, axis=0)
        lg_near = _dot_nt(qs, k_near) + bias_ref[n] + jnp.concatenate([neg_near] * GROUP, axis=0)
        m = jnp.maximum(jnp.max(lg_far, axis=1, keepdims=True), jnp.max(lg_near, axis=1, keepdims=True))
        p_far = jnp.exp(lg_far - m)
        p_near = jnp.exp(lg_near - m)
        l = jnp.sum(p_far, axis=1, keepdims=True) + jnp.sum(p_near, axis=1, keepdims=True)
        o = _dot(p_far.astype(BF16), v_far) + _dot(p_near.astype(BF16), v_near)
        outs.append(o / l)
    o_ref[0] = _unstack_heads(outs, t)


def _dsa_sample_attn(z, keys, thr, cut, cache_k, cache_v, page_table, layer, bias_s, cfar_s):
    b, t, _ = z.shape
    n_pages = page_table.shape[1]
    nblk = n_pages + 1
    grid_spec = pltpu.PrefetchScalarGridSpec(
        num_scalar_prefetch=1,
        grid=(b,),
        in_specs=[pl.BlockSpec((1, t, D_ATT), lambda bi, pt: (bi, 0, C_Q // D_ATT)),
                  pl.BlockSpec((1, t, LANES), lambda bi, pt: (bi, 0, C_K // LANES)),
                  pl.BlockSpec((1, t, LANES), lambda bi, pt: (bi, 0, C_VV // LANES)),
                  pl.BlockSpec((nblk, t, KEY_BLK), lambda bi, pt: (0, bi, 0)),
                  pl.BlockSpec((t, 1), lambda bi, pt: (bi, 0)),
                  pl.BlockSpec((t, 1), lambda bi, pt: (bi, 0)),
                  pl.BlockSpec((N_KV, GROUP * t, 2 * KEY_BLK), lambda bi, pt: (0, 0, 0)),
                  pl.BlockSpec((N_KV, GROUP * t, 1), lambda bi, pt: (0, 0, 0))]
        + _page_specs(n_pages, layer, LANES) + _page_specs(n_pages, layer, LANES),
        out_specs=pl.BlockSpec((1, t, D_ATT), lambda bi, pt: (bi, 0, 0)),
    )
    return pl.pallas_call(
        _dsa_sample_attn_kernel,
        grid_spec=grid_spec,
        out_shape=jax.ShapeDtypeStruct((b, t, D_ATT), F32),
        compiler_params=_cp(1),
        name="dsa_sample_attn",
    )(page_table, z, z, z, keys, thr, cut, bias_s, cfar_s, *([cache_k] * n_pages), *([cache_v] * n_pages))


def _merge_kernel(ya_ref, yb_ref, yc_ref, ga_ref, gb_ref, gc_ref, x_ref, g1_ref, sh2_ref, sc2_ref,
                  wpa_ref, wpb_ref, wpc_ref, wo_ref, lng_ref, lnb_ref, wrh_ref, wrl_ref, rb_ref,
                  x1_ref, h2_ref, gate_ref, *, alpha):
    gbk, rbk, d = x_ref.shape
    m = gbk * rbk

    def flat(ref):
        return ref[...].reshape(m, ref.shape[-1])

    merged = (_sigmoid(flat(ga_ref)) * _dot(flat(ya_ref).astype(BF16), wpa_ref[...])
              + _sigmoid(flat(gb_ref)) * _dot(flat(yb_ref).astype(BF16), wpb_ref[...])
              + _sigmoid(flat(gc_ref)) * _dot(flat(yc_ref).astype(BF16), wpc_ref[...]))
    mix = _dot(merged.astype(BF16), wo_ref[...]).reshape(gbk, rbk, d)
    x1 = _ln(alpha * x_ref[...] + (1.0 + g1_ref[...]) * mix, lng_ref[...], lnb_ref[...])
    x1_ref[...] = x1
    h2 = x1 * (1.0 + sc2_ref[...]) + sh2_ref[...]
    h2_ref[...] = h2.astype(h2_ref.dtype)

    h2f = h2.reshape(m, d)
    hi = h2f.astype(BF16)
    lo = (h2f - hi.astype(F32)).astype(BF16)
    logits = _dot_nt(wrh_ref[...], hi) + _dot_nt(wrl_ref[...], hi) + _dot_nt(wrh_ref[...], lo)
    scores = _sigmoid(logits)
    sel = scores + rb_ref[...]
    row = lax.broadcasted_iota(I32, (N_EXPERTS, m), 0).astype(F32)
    best = jnp.zeros((1, m), F32)
    best_score = None
    for g in range(N_GROUPS):
        r = [sel[g * EXPERTS_PER_GROUP + i:g * EXPERTS_PER_GROUP + i + 1] for i in range(EXPERTS_PER_GROUP)]
        top2 = None
        for i in range(EXPERTS_PER_GROUP):
            for j in range(i + 1, EXPERTS_PER_GROUP):
                pair = r[i] + r[j]
                top2 = pair if top2 is None else jnp.maximum(top2, pair)
        if best_score is None:
            best_score = top2
        else:
            better = top2 > best_score
            best = jnp.where(better, float(g), best)
            best_score = jnp.where(better, top2, best_score)
    lo_row = best * EXPERTS_PER_GROUP
    in_group = (row >= lo_row) & (row < lo_row + EXPERTS_PER_GROUP)
    masked = jnp.where(in_group, sel, -jnp.inf)
    m1 = jnp.max(masked, axis=0, keepdims=True)
    i1 = jnp.min(jnp.where(masked == m1, row, float(N_EXPERTS)), axis=0, keepdims=True)
    masked2 = jnp.where(row == i1, -jnp.inf, masked)
    m2 = jnp.max(masked2, axis=0, keepdims=True)
    i2 = jnp.min(jnp.where(masked2 == m2, row, float(N_EXPERTS)), axis=0, keepdims=True)
    w1 = jnp.sum(jnp.where(row == i1, scores, 0.0), axis=0, keepdims=True)
    w2 = jnp.sum(jnp.where(row == i2, scores, 0.0), axis=0, keepdims=True)
    tot = w1 + w2
    gate_ref[...] = jnp.where(row == i1, w1 / tot, 0.0) + jnp.where(row == i2, w2 / tot, 0.0)


def _merge(ya, yb, yc, z, x3, mod, w_pa, w_pb, w_pc, w_o, lng, lnb, wr_hi, wr_lo, rbias, gb, rb, act_dtype, alpha):
    g, r, d = x3.shape
    rt = r // rb

    def act(width, col):
        return pl.BlockSpec((gb, rb, width), lambda gi, ri: (gi, ri, col))

    def modspec(col):
        return pl.BlockSpec((gb, 1, d), lambda gi, ri: (gi, 0, col))

    def full(a):
        return pl.BlockSpec(a.shape, lambda gi, ri: (0,) * a.ndim)

    return pl.pallas_call(
        functools.partial(_merge_kernel, alpha=alpha),
        grid=(g // gb, rt),
        in_specs=[act(D_A, 0), act(D_ATT, 0), act(D_C, 0),
                  act(d, C_GA // d), act(d, C_GB // d), act(d, C_GC // d),
                  act(d, 0), modspec(2), modspec(3), modspec(4),
                  full(w_pa), full(w_pb), full(w_pc), full(w_o), full(lng), full(lnb),
                  full(wr_hi), full(wr_lo), full(rbias)],
        out_specs=[act(d, 0), act(d, 0),
                   pl.BlockSpec((N_EXPERTS, gb * rb), lambda gi, ri: (0, gi * rt + ri))],
        out_shape=[jax.ShapeDtypeStruct((g, r, d), F32),
                   jax.ShapeDtypeStruct((g, r, d), act_dtype),
                   jax.ShapeDtypeStruct((N_EXPERTS, g * r), F32)],
        compiler_params=_cp(2),
        name="merge",
    )(ya, yb, yc, z, z, z, x3, mod, mod, mod, w_pa, w_pb, w_pc, w_o, lng, lnb, wr_hi, wr_lo, rbias)


def _moe_kernel(h_ref, gate_ref, x1_ref, g2_ref, wg_ref, wu_ref, wd_ref, lng_ref, lnb_ref, o_ref, acc_scr, *, alpha):
    e = pl.program_id(2)
    gbk, rbk, d = h_ref.shape
    m = gbk * rbk

    @pl.when(e == 0)
    def _():
        acc_scr[...] = jnp.zeros(acc_scr.shape, F32)

    h = h_ref[...].reshape(m, d).astype(BF16)
    a = _dot(h, wg_ref[0, 0])
    u = _dot(h, wu_ref[0, 0])
    he = (a * _sigmoid(a)) * u
    gate = gate_ref[...]
    lane = lax.broadcasted_iota(I32, gate.shape, 1)
    gcol = jnp.sum(jnp.where(lane == e, gate, 0.0), axis=1, keepdims=True)
    acc_scr[...] += gcol * _dot(he.astype(BF16), wd_ref[0, 0])

    @pl.when(e == pl.num_programs(2) - 1)
    def _():
        y = acc_scr[...].reshape(gbk, rbk, d)
        o_ref[...] = _ln(alpha * x1_ref[...] + (1.0 + g2_ref[...]) * y, lng_ref[...], lnb_ref[...])


def _moe(h2, gate, x1, mod, w_g, w_u, w_d, lng, lnb, layer, gb, rb, alpha):
    g, r, d = x1.shape
    rt = r // rb
    n_e = w_g.shape[1]
    return pl.pallas_call(
        functools.partial(_moe_kernel, alpha=alpha),
        grid=(g // gb, rt, n_e),
        in_specs=[pl.BlockSpec((gb, rb, d), lambda gi, ri, e: (gi, ri, 0)),
                  pl.BlockSpec((gb * rb, n_e), lambda gi, ri, e: (gi * rt + ri, 0)),
                  pl.BlockSpec((gb, rb, d), lambda gi, ri, e: (gi, ri, 0)),
                  pl.BlockSpec((gb, 1, d), lambda gi, ri, e: (gi, 0, 5)),
                  pl.BlockSpec((1, 1, d, D_EXPERT), lambda gi, ri, e: (layer, e, 0, 0)),
                  pl.BlockSpec((1, 1, d, D_EXPERT), lambda gi, ri, e: (layer, e, 0, 0)),
                  pl.BlockSpec((1, 1, D_EXPERT, d), lambda gi, ri, e: (layer, e, 0, 0)),
                  pl.BlockSpec((1, d), lambda gi, ri, e: (0, 0)),
                  pl.BlockSpec((1, d), lambda gi, ri, e: (0, 0))],
        out_specs=pl.BlockSpec((gb, rb, d), lambda gi, ri, e: (gi, ri, 0)),
        out_shape=jax.ShapeDtypeStruct((g, r, d), F32),
        scratch_shapes=[pltpu.VMEM((gb * rb, d), F32)],
        compiler_params=_cp(3),
        name="moe",
    )(h2, gate, x1, mod, w_g, w_u, w_d, lng, lnb)


def _pad_cols(n, dtype, like):
    return jnp.zeros(like.shape[:-1] + (n,), dtype)


def _relayout_in(w):
    def sl(o, n):
        return w[..., o:o + n]
    ki = sl(_O_KI, D_IDX)
    parts = [sl(_O_AGLU, 2 * D_A), sl(_O_UV, 2 * D_C), sl(_O_G, 3 * D_MODEL), sl(_O_Q, D_ATT),
             sl(_O_QI, N_IDX_HEADS * D_IDX), sl(_O_K, N_KV * HEAD_DIM), sl(_O_V, N_KV * HEAD_DIM),
             ki, ki, ki, ki, sl(_O_WI, N_IDX_HEADS), _pad_cols(D_ZP - C_WI - N_IDX_HEADS, w.dtype, w)]
    return jnp.concatenate(parts, axis=-1)


def kernel(x_prompt, x_sample, cache_k, cache_v, cache_kidx, state_conv, page_table, c_prompt, c_sample,
           w_ada, b_ada, w_in, b_in, conv_k, conv_b, lnA_g, lnA_b, lnC_g, lnC_b, w_s, b_s,
           w_pa, w_pb, w_pc, w_o, ln1_g, ln1_b, ln2_g, ln2_b, rel_table, w_router, router_bias,
           w_e_gate, w_e_up, w_e_down):
    depth = w_in.shape[0]
    alpha = (2 * depth) ** 0.25
    bp, tp, d = x_prompt.shape
    bs, ts, _ = x_sample.shape
    n_pages = page_table.shape[1]
    past = n_pages * PAGE_SIZE
    topk_s = min(TOPK_MAX, (past + ts) // 4)
    n_pool = cache_k.shape[1]

    w_in_p = _relayout_in(w_in).astype(BF16)
    b_in_p = _relayout_in(b_in).reshape(depth, 1, D_ZP)
    w_pa_b, w_pb_b, w_pc_b, w_o_b = (w.astype(BF16) for w in (w_pa, w_pb, w_pc, w_o))
    w_g_b, w_u_b, w_d_b = (w.astype(BF16) for w in (w_e_gate, w_e_up, w_e_down))
    wr_t = w_router.T
    wr_hi = wr_t.astype(BF16)
    wr_lo = (wr_t - wr_hi.astype(F32)).astype(BF16)
    rbias = router_bias.reshape(N_EXPERTS, 1)
    grp_w = D_C // N_CGROUPS
    bs_tab_p = jnp.repeat(jnp.swapaxes(b_s[:, :, :CHUNK], 1, 2), grp_w, axis=2)
    bs_tab_s = bs_tab_p[:, :ts]
    wl_tab = jnp.repeat(jnp.transpose(w_s[:, :, :ts, :ts], (0, 3, 2, 1)), grp_w, axis=3)
    cache_k2 = cache_k.reshape(depth, n_pool, PAGE_SIZE, N_KV * HEAD_DIM)
    cache_v2 = cache_v.reshape(depth, n_pool, PAGE_SIZE, N_KV * HEAD_DIM)

    def row(a, l):
        return a[l].reshape(1, -1)

    bias_p = _bias_near(rel_table)
    bias_s = bias_p.reshape(N_KV, GROUP, KEY_BLK, 2 * KEY_BLK)[:, :, :ts].reshape(N_KV, GROUP * ts, 2 * KEY_BLK)
    far_h = rel_table[N_BUCKETS - 1].reshape(N_KV, GROUP, 1)
    cfar_p = jnp.repeat(far_h, KEY_BLK, axis=1).reshape(N_KV, GROUP * KEY_BLK, 1)
    cfar_s = jnp.repeat(far_h, ts, axis=1).reshape(N_KV, GROUP * ts, 1)

    n_c = bp + bs
    c_all = jnp.concatenate([c_prompt, c_sample, jnp.zeros((-n_c % 8, d), F32)], axis=0)
    mod_all = _ada(c_all, w_ada, b_ada)

    yp, ys = x_prompt, x_sample
    outs = {k: [] for k in ("kp", "vp", "kip", "cp", "ks", "vs", "kis", "cs", "vcs")}
    rb_p = 512
    for l in range(depth):
        mod_p = mod_all[l, :bp].reshape(bp, 1, 6 * d)
        mod_s = mod_all[l, bp:n_c].reshape(bs, 1, 6 * d)
        lnA = (row(lnA_g, l), row(lnA_b, l))
        lnC = (row(lnC_g, l), row(lnC_b, l))
        ln1 = (row(ln1_g, l), row(ln1_b, l))
        ln2 = (row(ln2_g, l), row(ln2_b, l))

        z = _inproj(yp, mod_p, w_in_p[l], b_in_p[l], 1, rb_p, 3328)
        ya, conv_state = _conv_prompt(z, conv_k[l], row(conv_b, l), *lnA)
        yb = _dsa_prompt(z, bias_p, cfar_p)
        yc = _sgu_prompt(z, *lnC, w_s[l], bs_tab_p[l])
        x1, h2, gate_t = _merge(ya, yb, yc, z, yp, mod_p, w_pa_b[l], w_pb_b[l], w_pc_b[l], w_o_b[l], *ln1,
                                wr_hi, wr_lo, rbias, 1, rb_p, BF16, alpha)
        yp = _moe(h2, gate_t.T, x1, mod_p, w_g_b, w_u_b, w_d_b, *ln2, l, 1, 1024, alpha)
        outs["kp"].append(z[:, :, C_K:C_K + N_KV * HEAD_DIM].reshape(bp, tp, N_KV, HEAD_DIM))
        outs["vp"].append(z[:, :, C_VV:C_VV + N_KV * HEAD_DIM].reshape(bp, tp, N_KV, HEAD_DIM))
        outs["kip"].append(z[:, :, C_KI4:C_KI4 + D_IDX])
        outs["cp"].append(conv_state)

        zs = _inproj(ys, mod_s, w_in_p[l], b_in_p[l], bs, ts, 1664)
        ya, conv_state = _conv_sample(zs, state_conv[l], conv_k[l], row(conv_b, l), *lnA)
        keys = _dsa_sample_scores(zs, cache_kidx, page_table, l)
        thr, cut = _dsa_sample_topk(keys, topk_s)
        yb = _dsa_sample_attn(zs, keys, thr, cut, cache_k2, cache_v2, page_table, l, bias_s, cfar_s)
        yc, v_chunk = _sgu_sample(zs, *lnC, wl_tab[l], bs_tab_s[l])
        x1, h2, gate_t = _merge(ya, yb, yc, zs, ys, mod_s, w_pa_b[l], w_pb_b[l], w_pc_b[l], w_o_b[l], *ln1,
                                wr_hi, wr_lo, rbias, bs, ts, F32, alpha)
        ys = _moe(h2, gate_t.T, x1, mod_s, w_g_b, w_u_b, w_d_b, *ln2, l, bs, ts, alpha)
        outs["ks"].append(zs[:, :, C_K:C_K + N_KV * HEAD_DIM].reshape(bs, ts, N_KV, HEAD_DIM))
        outs["vs"].append(zs[:, :, C_VV:C_VV + N_KV * HEAD_DIM].reshape(bs, ts, N_KV, HEAD_DIM))
        outs["kis"].append(zs[:, :, C_KI4:C_KI4 + D_IDX])
        outs["cs"].append(conv_state)
        outs["vcs"].append(v_chunk)

    st = {k: jnp.stack(v) for k, v in outs.items()}
    return (yp, ys, st["kp"], st["vp"], st["kip"], st["cp"],
            st["ks"], st["vs"], st["kis"], st["cs"], st["vcs"])
```

```python
import functools
import math

import jax
import jax.numpy as jnp
from jax import lax
from jax.experimental import pallas as pl
from jax.experimental.pallas import tpu as pltpu

F32 = jnp.float32
BF16 = jnp.bfloat16
I32 = jnp.int32

D_MODEL = 1024
D_A = 512
CONV_W = 31
N_HEADS = 8
N_KV = 2
GROUP = N_HEADS // N_KV
HEAD_DIM = 64
D_ATT = N_HEADS * HEAD_DIM
N_IDX_HEADS = 4
D_IDX = 64
TOPK_MAX = 256
N_BUCKETS = 32
MAX_DIST = 128
D_C = 512
N_CGROUPS = 8
CHUNK = 128
N_EXPERTS = 16
N_GROUPS = 4
EXPERTS_PER_GROUP = N_EXPERTS // N_GROUPS
D_EXPERT = 512
PAGE_SIZE = 128
LN_EPS = 1e-5

LANES = 128
KEY_BLK = 128
FAR_BLKS = 4
NEG_BIG = -1e30
LOG2E = math.log2(math.e)
INT_MIN = -2 ** 31
VMEM_LIMIT = 56 * 1024 * 1024

C_A, C_AG, C_U, C_V = 0, 512, 1024, 1536
C_GA, C_GB, C_GC = 2048, 3072, 4096
C_Q, C_QI, C_K, C_VV, C_KI4, C_WI = 5120, 5632, 5888, 6016, 6144, 6400
D_ZP = 6656

_O_AGLU, _O_Q, _O_K, _O_V, _O_QI, _O_KI, _O_WI, _O_UV, _O_G = 0, 1024, 1536, 1664, 1792, 2048, 2112, 2116, 3140
_D_IN = 6212


def _cp(n_axes):
    return pltpu.CompilerParams(dimension_semantics=("arbitrary",) * n_axes, vmem_limit_bytes=VMEM_LIMIT)


def _sigmoid(x):
    return jax.nn.sigmoid(x)


def _ln(x, g, b):
    mu = jnp.mean(x, axis=-1, keepdims=True)
    xc = x - mu
    var = jnp.mean(xc * xc, axis=-1, keepdims=True)
    return xc * lax.rsqrt(var + LN_EPS) * g + b


def _dot(a, b):
    return jnp.dot(a, b, preferred_element_type=F32)


def _dot_nt(a, b):
    return lax.dot_general(a, b, (((1,), (1,)), ((), ())), preferred_element_type=F32)


def _ada_kernel(c_ref, w_ref, b_ref, o_ref):
    c = c_ref[...]
    s = (c * _sigmoid(c)).astype(BF16)
    o_ref[0] = _dot(s, w_ref[0].astype(BF16)) + b_ref[0]


def _ada(c_all, w_ada, b_ada):
    depth, d, n = w_ada.shape
    m = c_all.shape[0]
    tn = 1536
    return pl.pallas_call(
        _ada_kernel,
        grid=(depth, n // tn),
        in_specs=[pl.BlockSpec((m, d), lambda l, j: (0, 0)),
                  pl.BlockSpec((1, d, tn), lambda l, j: (l, 0, j)),
                  pl.BlockSpec((1, 1, tn), lambda l, j: (l, 0, j))],
        out_specs=pl.BlockSpec((1, m, tn), lambda l, j: (l, 0, j)),
        out_shape=jax.ShapeDtypeStruct((depth, m, n), F32),
        compiler_params=_cp(2),
        name="ada",
    )(c_all, w_ada, b_ada.reshape(depth, 1, n))


def _inproj_kernel(x_ref, sh_ref, sc_ref, w_ref, b_ref, z_ref):
    gb, rb, d = x_ref.shape
    h = x_ref[...] * (1.0 + sc_ref[...]) + sh_ref[...]
    z = _dot(h.reshape(gb * rb, d).astype(BF16), w_ref[...]) + b_ref[...]
    z_ref[...] = z.reshape(gb, rb, z.shape[-1])


def _inproj(x3, mod, w, b, gb, rb, tn):
    g, r, d = x3.shape
    n = w.shape[1]
    return pl.pallas_call(
        _inproj_kernel,
        grid=(n // tn, g // gb, r // rb),
        in_specs=[pl.BlockSpec((gb, rb, d), lambda j, gi, ri: (gi, ri, 0)),
                  pl.BlockSpec((gb, 1, d), lambda j, gi, ri: (gi, 0, 0)),
                  pl.BlockSpec((gb, 1, d), lambda j, gi, ri: (gi, 0, 1)),
                  pl.BlockSpec((d, tn), lambda j, gi, ri: (0, j)),
                  pl.BlockSpec((1, tn), lambda j, gi, ri: (0, j))],
        out_specs=pl.BlockSpec((gb, rb, tn), lambda j, gi, ri: (gi, ri, j)),
        out_shape=jax.ShapeDtypeStruct((g, r, n), F32),
        compiler_params=_cp(3),
        name="inproj",
    )(x3, mod, mod, w, b)


CONV_HALO = 32
CONV_ROWS = 32


def _conv_prompt_kernel(a_ref, g_ref, ah_ref, gh_ref, ck_ref, cb_ref, lg_ref, lb_ref, y_ref, st_ref, ext_scr):
    t = pl.program_id(1)
    tt = a_ref.shape[1]
    a = a_ref[0] * _sigmoid(g_ref[0])
    halo = ah_ref[0] * _sigmoid(gh_ref[0])
    ext_scr[0:CONV_HALO, :] = jnp.where(t > 0, halo, 0.0)
    ext_scr[CONV_HALO:, :] = a
    off = CONV_HALO - (CONV_W - 1)
    for c in range(tt // CONV_ROWS):
        r0 = c * CONV_ROWS
        acc = jnp.zeros((CONV_ROWS, D_A), F32) + cb_ref[...]
        for j in range(CONV_W):
            acc = acc + ck_ref[j:j + 1, :] * ext_scr[r0 + off + j:r0 + off + j + CONV_ROWS, :]
        y = _ln(acc, lg_ref[...], lb_ref[...])
        y_ref[0, r0:r0 + CONV_ROWS, :] = (y * _sigmoid(y)).astype(y_ref.dtype)

    @pl.when(t == pl.num_programs(1) - 1)
    def _():
        st_ref[0] = ext_scr[tt + off:tt + CONV_HALO, :]


def _conv_prompt(z, ck, cb, lg, lb, tt=256):
    b, t, _ = z.shape
    hb = tt // CONV_HALO
    return pl.pallas_call(
        _conv_prompt_kernel,
        grid=(b, t // tt),
        in_specs=[pl.BlockSpec((1, tt, D_A), lambda bi, ti: (bi, ti, C_A // D_A)),
                  pl.BlockSpec((1, tt, D_A), lambda bi, ti: (bi, ti, C_AG // D_A)),
                  pl.BlockSpec((1, CONV_HALO, D_A), lambda bi, ti: (bi, jnp.maximum(ti * hb - 1, 0), C_A // D_A)),
                  pl.BlockSpec((1, CONV_HALO, D_A), lambda bi, ti: (bi, jnp.maximum(ti * hb - 1, 0), C_AG // D_A)),
                  pl.BlockSpec((CONV_W, D_A), lambda bi, ti: (0, 0)),
                  pl.BlockSpec((1, D_A), lambda bi, ti: (0, 0)),
                  pl.BlockSpec((1, D_A), lambda bi, ti: (0, 0)),
                  pl.BlockSpec((1, D_A), lambda bi, ti: (0, 0))],
        out_specs=[pl.BlockSpec((1, tt, D_A), lambda bi, ti: (bi, ti, 0)),
                   pl.BlockSpec((1, CONV_W - 1, D_A), lambda bi, ti: (bi, 0, 0))],
        out_shape=[jax.ShapeDtypeStruct((b, t, D_A), BF16),
                   jax.ShapeDtypeStruct((b, CONV_W - 1, D_A), F32)],
        scratch_shapes=[pltpu.VMEM((tt + CONV_HALO, D_A), F32)],
        compiler_params=_cp(2),
        name="conv_prompt",
    )(z, z, z, z, ck, cb, lg, lb)


def _conv_sample_kernel(a_ref, g_ref, st_ref, ck_ref, cb_ref, lg_ref, lb_ref, y_ref, ns_ref, ext_scr):
    bs, t, _ = a_ref.shape
    hist = CONV_W - 1
    ext_scr[:, 0:hist, :] = st_ref[...]
    ext_scr[:, hist:hist + t, :] = a_ref[...] * _sigmoid(g_ref[...])
    ns_ref[...] = ext_scr[:, t:t + hist, :]

    def body(b, carry):
        acc = jnp.zeros((t, D_A), F32) + cb_ref[...]
        for j in range(CONV_W):
            acc = acc + ck_ref[j:j + 1, :] * ext_scr[b, j:j + t, :]
        y = _ln(acc, lg_ref[...], lb_ref[...])
        y_ref[b] = y * _sigmoid(y)
        return carry

    lax.fori_loop(0, bs, body, 0)


def _conv_sample(z, state, ck, cb, lg, lb, bs=32):
    b, t, _ = z.shape
    hist = CONV_W - 1
    bs = min(bs, b)
    return pl.pallas_call(
        _conv_sample_kernel,
        grid=(b // bs,),
        in_specs=[pl.BlockSpec((bs, t, D_A), lambda bi: (bi, 0, C_A // D_A)),
                  pl.BlockSpec((bs, t, D_A), lambda bi: (bi, 0, C_AG // D_A)),
                  pl.BlockSpec((bs, hist, D_A), lambda bi: (bi, 0, 0)),
                  pl.BlockSpec((CONV_W, D_A), lambda bi: (0, 0)),
                  pl.BlockSpec((1, D_A), lambda bi: (0, 0)),
                  pl.BlockSpec((1, D_A), lambda bi: (0, 0)),
                  pl.BlockSpec((1, D_A), lambda bi: (0, 0))],
        out_specs=[pl.BlockSpec((bs, t, D_A), lambda bi: (bi, 0, 0)),
                   pl.BlockSpec((bs, hist, D_A), lambda bi: (bi, 0, 0))],
        out_shape=[jax.ShapeDtypeStruct((b, t, D_A), F32),
                   jax.ShapeDtypeStruct((b, hist, D_A), F32)],
        scratch_shapes=[pltpu.VMEM((bs, hist + t + 2, D_A), F32)],
        compiler_params=_cp(1),
        name="conv_sample",
    )(z, z, state, ck, cb, lg, lb)


def _sgu_prompt_kernel(u_ref, v_ref, lg_ref, lb_ref, ws_ref, bs_ref, y_ref):
    tt = u_ref.shape[1]
    vv = _ln(v_ref[0], lg_ref[...], lb_ref[...])
    row = lax.broadcasted_iota(I32, (CHUNK, CHUNK), 0)
    col = lax.broadcasted_iota(I32, (CHUNK, CHUNK), 1)
    lane_grp = lax.broadcasted_iota(I32, (CHUNK, D_C), 1) // (D_C // N_CGROUPS)
    ws = [jnp.where(col <= row, ws_ref[g], 0.0).astype(BF16) for g in range(N_CGROUPS)]
    for c in range(tt // CHUNK):
        vc = vv[c * CHUNK:(c + 1) * CHUNK].astype(BF16)
        mixed = bs_ref[...]
        for g in range(N_CGROUPS):
            mixed = mixed + _dot(ws[g], jnp.where(lane_grp == g, vc, jnp.zeros_like(vc)))
        y_ref[0, c * CHUNK:(c + 1) * CHUNK, :] = (u_ref[0, c * CHUNK:(c + 1) * CHUNK, :] * mixed).astype(y_ref.dtype)


def _sgu_prompt(z, lg, lb, w_s, bs_tab, tt=512):
    b, t, _ = z.shape
    return pl.pallas_call(
        _sgu_prompt_kernel,
        grid=(b, t // tt),
        in_specs=[pl.BlockSpec((1, tt, D_C), lambda bi, ti: (bi, ti, C_U // D_C)),
                  pl.BlockSpec((1, tt, D_C), lambda bi, ti: (bi, ti, C_V // D_C)),
                  pl.BlockSpec((1, D_C), lambda bi, ti: (0, 0)),
                  pl.BlockSpec((1, D_C), lambda bi, ti: (0, 0)),
                  pl.BlockSpec((N_CGROUPS, CHUNK, CHUNK), lambda bi, ti: (0, 0, 0)),
                  pl.BlockSpec((CHUNK, D_C), lambda bi, ti: (0, 0))],
        out_specs=pl.BlockSpec((1, tt, D_C), lambda bi, ti: (bi, ti, 0)),
        out_shape=jax.ShapeDtypeStruct((b, t, D_C), BF16),
        compiler_params=_cp(2),
        name="sgu_prompt",
    )(z, z, lg, lb, w_s, bs_tab)


def _sgu_sample_kernel(u_ref, v_ref, lg_ref, lb_ref, wl_ref, bs_ref, y_ref, vc_ref):
    bs, t, _ = u_ref.shape
    vv = _ln(v_ref[...], lg_ref[...], lb_ref[...])
    vc_ref[...] = vv
    tpos = lax.broadcasted_iota(I32, (t, D_C), 0)
    mixed = jnp.zeros((bs, t, D_C), F32) + bs_ref[...]
    for s in range(t):
        w = jnp.where(tpos >= s, wl_ref[s], 0.0)
        mixed = mixed + w * vv[:, s:s + 1, :]
    y_ref[...] = u_ref[...] * mixed


def _sgu_sample(z, lg, lb, wl_tab, bs_tab):
    b, t, _ = z.shape
    return pl.pallas_call(
        _sgu_sample_kernel,
        grid=(1,),
        in_specs=[pl.BlockSpec((b, t, D_C), lambda i: (0, 0, C_U // D_C)),
                  pl.BlockSpec((b, t, D_C), lambda i: (0, 0, C_V // D_C)),
                  pl.BlockSpec((1, D_C), lambda i: (0, 0)),
                  pl.BlockSpec((1, D_C), lambda i: (0, 0)),
                  pl.BlockSpec((t, t, D_C), lambda i: (0, 0, 0)),
                  pl.BlockSpec((t, D_C), lambda i: (0, 0))],
        out_specs=[pl.BlockSpec((b, t, D_C), lambda i: (0, 0, 0)),
                   pl.BlockSpec((b, t, D_C), lambda i: (0, 0, 0))],
        out_shape=[jax.ShapeDtypeStruct((b, t, D_C), F32),
                   jax.ShapeDtypeStruct((b, t, D_C), F32)],
        compiler_params=_cp(1),
        name="sgu_sample",
    )(z, z, lg, lb, wl_tab, bs_tab)


def _bias_kernel(tab_ref, o_ref):
    rows, cols = KEY_BLK, 2 * KEY_BLK
    i = lax.broadcasted_iota(I32, (rows, cols), 0)
    j = lax.broadcasted_iota(I32, (rows, cols), 1)
    n = jnp.maximum(KEY_BLK + i - j, 0)
    max_exact = N_BUCKETS // 2
    nf = jnp.maximum(n, 1).astype(F32)
    large = max_exact + (jnp.log(nf / max_exact) / math.log(MAX_DIST / max_exact) * (N_BUCKETS - max_exact)).astype(I32)
    large = jnp.minimum(large, N_BUCKETS - 1)
    bucket = jnp.where(n < max_exact, n, large)
    for h in range(N_HEADS):
        acc = jnp.zeros((rows, cols), F32)
        for bkt in range(N_BUCKETS):
            acc = jnp.where(bucket == bkt, tab_ref[bkt, h], acc)
        o_ref[h // GROUP, (h % GROUP) * rows:(h % GROUP + 1) * rows, :] = (acc - tab_ref[N_BUCKETS - 1, h]) * LOG2E


def _bias_near(rel_table):
    return pl.pallas_call(
        _bias_kernel,
        grid=(1,),
        in_specs=[pl.BlockSpec(memory_space=pltpu.SMEM)],
        out_specs=pl.BlockSpec((N_KV, GROUP * KEY_BLK, 2 * KEY_BLK), lambda i: (0, 0, 0)),
        out_shape=jax.ShapeDtypeStruct((N_KV, GROUP * KEY_BLK, 2 * KEY_BLK), F32),
        compiler_params=_cp(1),
        name="bias_near",
    )(rel_table)


def _score_keys(s):
    s = jnp.where(s == 0.0, 0.0, s)
    bits = lax.bitcast_convert_type(s, I32)
    return bits ^ (jnp.right_shift(bits, 31) & 0x7FFFFFFF)


def _lane_top2(s, m1, m2):
    for blk in [s[:, c * LANES:(c + 1) * LANES] for c in range(s.shape[1] // LANES)]:
        m2 = jnp.maximum(m2, jnp.minimum(m1, blk))
        m1 = jnp.maximum(m1, blk)
    return m1, m2


def _topk_threshold(load_blk, n_iter, unroll, rows, k, idx_bits, cut_scr, bounds=None):
    lane = lax.broadcasted_iota(I32, (rows, KEY_BLK), 1)
    if bounds is None:
        first_bit = 0
        thr0 = jnp.full((rows, 1), INT_MIN, I32)
    else:
        lo, hi = bounds
        first_bit = jnp.minimum(jnp.min(lax.clz(lo ^ hi)), 31)
        keep = ~(jnp.left_shift(jnp.int32(2), 31 - first_bit) - 1)
        thr0 = ((hi ^ INT_MIN) & keep) ^ INT_MIN

    def count(pred):
        def body(jj, acc):
            for u in range(unroll):
                j = jj * unroll + u
                acc = acc + jnp.where(pred(load_blk(j), j), 1.0, 0.0)
            return acc
        acc = lax.fori_loop(0, n_iter, body, jnp.zeros((rows, KEY_BLK), F32))
        return jnp.sum(acc, axis=1, keepdims=True)

    kf = float(k)

    def bit_body(i, thr):
        cand = thr ^ jnp.left_shift(jnp.int32(1), 31 - i)
        cnt = count(lambda kb, j: kb >= cand)
        return jnp.where(cnt >= kf, cand, thr)

    thr = lax.fori_loop(first_bit, 32, bit_body, thr0)
    cnt_ge = count(lambda kb, j: kb >= thr)
    cnt_gt = count(lambda kb, j: kb > thr)
    need = kf - cnt_gt
    cut_scr[...] = jnp.full((rows, 1), 2 ** idx_bits, I32)

    @pl.when(jnp.max(cnt_ge) > kf)
    def _():
        def idx_body(i, r):
            cand = r + jnp.left_shift(jnp.int32(1), idx_bits - 1 - i)
            cnt = count(lambda kb, j: (kb == thr) & (j * KEY_BLK + lane < cand))
            return jnp.where(cnt < need, cand, r)
        r = lax.fori_loop(0, idx_bits, idx_body, jnp.zeros((rows, 1), I32))
        cut_scr[...] = r + 1

    return thr, cut_scr[...]


def _select_mask(keys, thr, cut, gidx):
    return (keys > thr) | ((keys == thr) & (gidx < cut))


def _stack_heads(q, n, rows):
    half = lax.broadcasted_iota(I32, (rows, LANES), 1) // HEAD_DIM
    tiles = []
    for g in range(GROUP):
        h = n * GROUP + g
        col = q[:, (h // 2) * LANES:(h // 2 + 1) * LANES]
        if h % 2 != n:
            col = pltpu.roll(col, HEAD_DIM, 1)
        tiles.append(jnp.where(half == n, col, 0.0))
    return jnp.concatenate(tiles, axis=0)


def _unstack_heads(o_groups, rows):
    half = lax.broadcasted_iota(I32, (rows, LANES), 1) // HEAD_DIM
    cols = []
    for c in range(N_HEADS // 2):
        parts = []
        for h in (2 * c, 2 * c + 1):
            n, g = h // GROUP, h % GROUP
            tile = o_groups[n][g * rows:(g + 1) * rows, :]
            if h % 2 != n:
                tile = pltpu.roll(tile, HEAD_DIM, 1)
            parts.append(tile)
        cols.append(jnp.where(half == 0, parts[0], parts[1]))
    return jnp.concatenate(cols, axis=1)


def _dsa_prompt_kernel(q_ref, qi_ref, wi_ref, ki_ref, k_ref, v_ref, bias_ref, o_ref,
                       key_scr, cut_scr, qs_scr, mpart_scr, mfull_scr, lpart_scr, acc_scr, p_scr,
                       lgfar_scr, lgnear_scr, *, topk):
    qb = pl.program_id(1)
    rows = KEY_BLK
    chunk = FAR_BLKS * KEY_BLK
    row_i = lax.broadcasted_iota(I32, (rows, 1), 0)
    tpos = qb * rows + row_i

    qi = qi_ref[0]
    lane_head = lax.broadcasted_iota(I32, qi.shape, 1) // D_IDX
    qim = [jnp.where(lane_head == h, qi, 0.0).astype(BF16) for h in range(N_IDX_HEADS)]
    wi = wi_ref[0]
    lane_c = lax.broadcasted_iota(I32, (rows, chunk), 1)
    n_chunks = qb // FAR_BLKS + 1

    def score_body(c, top2):
        kc = ki_ref[0, pl.ds(pl.multiple_of(c * chunk, chunk), chunk), :].astype(BF16)
        s = jnp.zeros((rows, chunk), F32)
        for h in range(N_IDX_HEADS):
            s = s + wi[:, h:h + 1] * jnp.maximum(_dot_nt(qim[h], kc), 0.0)
        s = jnp.where(c * chunk + lane_c <= tpos, s, -jnp.inf)
        keys = _score_keys(s)
        for j in range(FAR_BLKS):
            key_scr[c * FAR_BLKS + j] = keys[:, j * KEY_BLK:(j + 1) * KEY_BLK]
        return _lane_top2(s, *top2)

    ninf = jnp.full((rows, LANES), -jnp.inf, F32)
    m1, m2 = lax.fori_loop(0, n_chunks, score_body, (ninf, ninf))

    assert topk <= 2 * LANES
    bounds = (_score_keys(jnp.min(m2, axis=1, keepdims=True)), _score_keys(jnp.max(m1, axis=1, keepdims=True)))
    thr, cut = _topk_threshold(lambda j: key_scr[j], n_chunks, FAR_BLKS, rows, topk, 13, cut_scr, bounds)

    q = q_ref[0] * (HEAD_DIM ** -0.5 * LOG2E)
    for n in range(N_KV):
        qs_scr[n] = _stack_heads(q, n, rows).astype(BF16)
    mpart_scr[...] = jnp.full(mpart_scr.shape, NEG_BIG, F32)
    lpart_scr[...] = jnp.zeros(lpart_scr.shape, F32)
    acc_scr[...] = jnp.zeros(acc_scr.shape, F32)

    def lane_blocks(t):
        return [t[:, c * LANES:(c + 1) * LANES] for c in range(t.shape[1] // LANES)]

    def fold(vals, op):
        while len(vals) > 1:
            vals = [op(vals[i], vals[i + 1]) for i in range(0, len(vals), 2)]
        return vals[0]

    def sweep_max(kc, neg, with_bias, lg_ref):
        for n in range(N_KV):
            lg = _dot_nt(qs_scr[n], kc)
            for g in range(GROUP):
                sl = slice(g * rows, (g + 1) * rows)
                t = lg[sl] + neg
                if with_bias:
                    t = t + bias_ref[n, sl, :]
                lg_ref[n, sl, :] = t
                mpart_scr[n, sl, :] = jnp.maximum(mpart_scr[n, sl, :], fold(lane_blocks(t), jnp.maximum))

    def sweep_acc(vc, lg_ref):
        width = vc.shape[0]
        for n in range(N_KV):
            for g in range(GROUP):
                sl = slice(g * rows, (g + 1) * rows)
                mf = mfull_scr[n, sl, :]
                ps = [jnp.exp2(tb - mf) for tb in lane_blocks(lg_ref[n, sl, :])]
                lpart_scr[n, sl, :] = lpart_scr[n, sl, :] + fold(ps, jnp.add)
                p_scr[sl, 0:width] = jnp.concatenate(ps, axis=1).astype(BF16)
            acc_scr[n] = acc_scr[n] + _dot(p_scr[:, 0:width], vc)

    far_end = (qb - 1) * rows
    n_far = (qb + 2) // FAR_BLKS

    def far_max(c, carry):
        start = pl.multiple_of(c * chunk, chunk)
        kc = k_ref[0, pl.ds(start, chunk), :].astype(BF16)
        keys = jnp.concatenate([key_scr[c * FAR_BLKS + j] for j in range(FAR_BLKS)], axis=1)
        gidx = c * chunk + lane_c
        sel = _select_mask(keys, thr, cut, gidx) & (gidx < far_end)
        sweep_max(kc, jnp.where(sel, 0.0, NEG_BIG), False, lgfar_scr.at[c])
        return carry

    lax.fori_loop(0, n_far, far_max, 0)

    pb = jnp.maximum(qb - 1, 0)
    p0 = pl.multiple_of(pb * rows, rows)
    q0 = pl.multiple_of(qb * rows, rows)
    lane_b = lax.broadcasted_iota(I32, (rows, KEY_BLK), 1)
    g_prev = pb * rows + lane_b
    g_diag = qb * rows + lane_b
    sel_prev = _select_mask(key_scr[pb], thr, cut, g_prev) & (qb > 0)
    sel_diag = _select_mask(key_scr[qb], thr, cut, g_diag) & (g_diag <= tpos)
    neg_near = jnp.concatenate([jnp.where(sel_prev, 0.0, NEG_BIG), jnp.where(sel_diag, 0.0, NEG_BIG)], axis=1)

    def near_kv(ref):
        return jnp.concatenate([ref[0, pl.ds(p0, rows), :], ref[0, pl.ds(q0, rows), :]], axis=0).astype(BF16)

    sweep_max(near_kv(k_ref), neg_near, True, lgnear_scr)
    for n in range(N_KV):
        mfull_scr[n] = jnp.broadcast_to(jnp.max(mpart_scr[n], axis=1, keepdims=True), mfull_scr.shape[1:])

    def far_acc(c, carry):
        start = pl.multiple_of(c * chunk, chunk)
        sweep_acc(v_ref[0, pl.ds(start, chunk), :].astype(BF16), lgfar_scr.at[c])
        return carry

    lax.fori_loop(0, n_far, far_acc, 0)
    sweep_acc(near_kv(v_ref), lgnear_scr)

    outs = [acc_scr[n] / jnp.sum(lpart_scr[n], axis=1, keepdims=True) for n in range(N_KV)]
    o_ref[0] = _unstack_heads(outs, rows).astype(o_ref.dtype)


def _dsa_prompt(z, bias_near):
    b, t, _ = z.shape
    nb = t // KEY_BLK
    assert nb % FAR_BLKS == 0
    topk = min(TOPK_MAX, t // 4)
    grows = GROUP * KEY_BLK
    chunk = FAR_BLKS * KEY_BLK
    return pl.pallas_call(
        functools.partial(_dsa_prompt_kernel, topk=topk),
        grid=(b, nb),
        in_specs=[pl.BlockSpec((1, KEY_BLK, D_ATT), lambda bi, qi: (bi, qi, C_Q // D_ATT)),
                  pl.BlockSpec((1, KEY_BLK, 256), lambda bi, qi: (bi, qi, C_QI // 256)),
                  pl.BlockSpec((1, KEY_BLK, LANES), lambda bi, qi: (bi, qi, C_WI // LANES)),
                  pl.BlockSpec((1, t, 256), lambda bi, qi: (bi, 0, C_KI4 // 256)),
                  pl.BlockSpec((1, t, LANES), lambda bi, qi: (bi, 0, C_K // LANES)),
                  pl.BlockSpec((1, t, LANES), lambda bi, qi: (bi, 0, C_VV // LANES)),
                  pl.BlockSpec((N_KV, grows, 2 * KEY_BLK), lambda bi, qi: (0, 0, 0))],
        out_specs=pl.BlockSpec((1, KEY_BLK, D_ATT), lambda bi, qi: (bi, qi, 0)),
        out_shape=jax.ShapeDtypeStruct((b, t, D_ATT), BF16),
        scratch_shapes=[pltpu.VMEM((nb, KEY_BLK, KEY_BLK), I32),
                        pltpu.VMEM((KEY_BLK, 1), I32),
                        pltpu.VMEM((N_KV, grows, LANES), BF16),
                        pltpu.VMEM((N_KV, grows, LANES), F32),
                        pltpu.VMEM((N_KV, grows, LANES), F32),
                        pltpu.VMEM((N_KV, grows, LANES), F32),
                        pltpu.VMEM((N_KV, grows, LANES), F32),
                        pltpu.VMEM((grows, chunk), BF16),
                        pltpu.VMEM((nb // FAR_BLKS, N_KV, grows, chunk), F32),
                        pltpu.VMEM((N_KV, grows, 2 * KEY_BLK), F32)],
        compiler_params=_cp(2),
        name="dsa_prompt",
    )(z, z, z, z, z, z, bias_near)


def _dsa_sample_score_kernel(pt_ref, qi_ref, wi_ref, kin_ref, *rest):
    n_pages = len(rest) - 1
    pages, o_ref = rest[:n_pages], rest[n_pages]
    t = qi_ref.shape[1]
    qi = qi_ref[0]
    wi = wi_ref[0]
    qs = jnp.concatenate([qi[:, h * D_IDX:(h + 1) * D_IDX] for h in range(N_IDX_HEADS)], axis=0).astype(BF16)
    k_past = jnp.concatenate([pg[0, 0] for pg in pages], axis=0).astype(BF16)
    k_new = jnp.concatenate([kin_ref[0][:, :D_IDX], jnp.zeros((KEY_BLK - t, D_IDX), F32)], axis=0).astype(BF16)

    def score(d):
        s = jnp.zeros((t, d.shape[1]), F32)
        for h in range(N_IDX_HEADS):
            s = s + wi[:, h:h + 1] * jnp.maximum(d[h * t:(h + 1) * t], 0.0)
        return s

    keys_past = _score_keys(score(_dot_nt(qs, k_past)))
    for j in range(n_pages):
        o_ref[j] = keys_past[:, j * KEY_BLK:(j + 1) * KEY_BLK]
    s_new = score(_dot_nt(qs, k_new))
    lane = lax.broadcasted_iota(I32, s_new.shape, 1)
    trow = lax.broadcasted_iota(I32, s_new.shape, 0)
    o_ref[n_pages] = _score_keys(jnp.where(lane <= trow, s_new, -jnp.inf))


def _page_specs(n_pages, layer, width):
    return [pl.BlockSpec((1, 1, PAGE_SIZE, width), lambda bi, pt, j=j: (layer, pt[bi, j], 0, 0))
            for j in range(n_pages)]


def _dsa_sample_scores(z, cache_kidx, page_table, layer):
    b, t, _ = z.shape
    n_pages = page_table.shape[1]
    nblk = n_pages + 1
    grid_spec = pltpu.PrefetchScalarGridSpec(
        num_scalar_prefetch=1,
        grid=(b,),
        in_specs=[pl.BlockSpec((1, t, 256), lambda bi, pt: (bi, 0, C_QI // 256)),
                  pl.BlockSpec((1, t, LANES), lambda bi, pt: (bi, 0, C_WI // LANES)),
                  pl.BlockSpec((1, t, 256), lambda bi, pt: (bi, 0, C_KI4 // 256))]
        + _page_specs(n_pages, layer, D_IDX),
        out_specs=pl.BlockSpec((nblk, t, KEY_BLK), lambda bi, pt: (0, bi, 0)),
    )
    return pl.pallas_call(
        _dsa_sample_score_kernel,
        grid_spec=grid_spec,
        out_shape=jax.ShapeDtypeStruct((nblk, b * t, KEY_BLK), I32),
        compiler_params=_cp(1),
        name="dsa_sample_scores",
    )(page_table, z, z, z, *([cache_kidx] * n_pages))


def _dsa_sample_topk_kernel(key_ref, thr_ref, cut_ref, cut_scr, *, topk):
    nblk, rows, _ = key_ref.shape
    thr, cut = _topk_threshold(lambda j: key_ref[j], 1, nblk, rows, topk, 13, cut_scr)
    thr_ref[...] = thr
    cut_ref[...] = cut


def _dsa_sample_topk(keys, topk, rows=128):
    nblk, m, _ = keys.shape
    return pl.pallas_call(
        functools.partial(_dsa_sample_topk_kernel, topk=topk),
        grid=(m // rows,),
        in_specs=[pl.BlockSpec((nblk, rows, KEY_BLK), lambda i: (0, i, 0))],
        out_specs=[pl.BlockSpec((rows, 1), lambda i: (i, 0)),
                   pl.BlockSpec((rows, 1), lambda i: (i, 0))],
        out_shape=[jax.ShapeDtypeStruct((m, 1), I32), jax.ShapeDtypeStruct((m, 1), I32)],
        scratch_shapes=[pltpu.VMEM((rows, 1), I32)],
        compiler_params=_cp(1),
        name="dsa_sample_topk",
    )(keys)


def _dsa_sample_attn_kernel(pt_ref, q_ref, kn_ref, vn_ref, key_ref, thr_ref, cut_ref, bias_ref, *rest):
    n_pages = (len(rest) - 1) // 2
    kpages, vpages, o_ref = rest[:n_pages], rest[n_pages:2 * n_pages], rest[2 * n_pages]
    t = q_ref.shape[1]
    past = n_pages * PAGE_SIZE
    thr = thr_ref[...]
    cut = cut_ref[...]
    q = q_ref[0] * (HEAD_DIM ** -0.5 * LOG2E)
    trow = lax.broadcasted_iota(I32, (t, 1), 0)

    n_far = n_pages - 1
    k_far = jnp.concatenate([pg[0, 0] for pg in kpages[:n_far]], axis=0).astype(BF16)
    v_far = jnp.concatenate([pg[0, 0] for pg in vpages[:n_far]], axis=0).astype(BF16)
    keys_far = jnp.concatenate([key_ref[j] for j in range(n_far)], axis=1)
    g_far = lax.broadcasted_iota(I32, keys_far.shape, 1)
    neg_far = jnp.where(_select_mask(keys_far, thr, cut, g_far), 0.0, NEG_BIG)

    pad = jnp.zeros((KEY_BLK - t, LANES), F32)
    k_near = jnp.concatenate([kpages[n_far][0, 0], kn_ref[0], pad], axis=0).astype(BF16)
    v_near = jnp.concatenate([vpages[n_far][0, 0], vn_ref[0], pad], axis=0).astype(BF16)
    lane_b = lax.broadcasted_iota(I32, (t, KEY_BLK), 1)
    sel_last = _select_mask(key_ref[n_far], thr, cut, n_far * KEY_BLK + lane_b)
    sel_new = _select_mask(key_ref[n_pages], thr, cut, past + lane_b) & (lane_b <= trow)
    neg_near = jnp.concatenate([jnp.where(sel_last, 0.0, NEG_BIG), jnp.where(sel_new, 0.0, NEG_BIG)], axis=1)

    outs = []
    for n in range(N_KV):
        qs = _stack_heads(q, n, t).astype(BF16)
        lg_far = _dot_nt(qs, k_far) + jnp.concatenate([neg_far] * GROUP, axis=0)
        lg_near = _dot_nt(qs, k_near) + bias_ref[n] + jnp.concatenate([neg_near] * GROUP, axis=0)
        m = jnp.maximum(jnp.max(lg_far, axis=1, keepdims=True), jnp.max(lg_near, axis=1, keepdims=True))
        p_far = jnp.exp2(lg_far - m)
        p_near = jnp.exp2(lg_near - m)
        l = jnp.sum(p_far, axis=1, keepdims=True) + jnp.sum(p_near, axis=1, keepdims=True)
        o = _dot(p_far.astype(BF16), v_far) + _dot(p_near.astype(BF16), v_near)
        outs.append(o / l)
    o_ref[0] = _unstack_heads(outs, t)


def _dsa_sample_attn(z, keys, thr, cut, cache_k, cache_v, page_table, layer, bias_s):
    b, t, _ = z.shape
    n_pages = page_table.shape[1]
    nblk = n_pages + 1
    grid_spec = pltpu.PrefetchScalarGridSpec(
        num_scalar_prefetch=1,
        grid=(b,),
        in_specs=[pl.BlockSpec((1, t, D_ATT), lambda bi, pt: (bi, 0, C_Q // D_ATT)),
                  pl.BlockSpec((1, t, LANES), lambda bi, pt: (bi, 0, C_K // LANES)),
                  pl.BlockSpec((1, t, LANES), lambda bi, pt: (bi, 0, C_VV // LANES)),
                  pl.BlockSpec((nblk, t, KEY_BLK), lambda bi, pt: (0, bi, 0)),
                  pl.BlockSpec((t, 1), lambda bi, pt: (bi, 0)),
                  pl.BlockSpec((t, 1), lambda bi, pt: (bi, 0)),
                  pl.BlockSpec((N_KV, GROUP * t, 2 * KEY_BLK), lambda bi, pt: (0, 0, 0))]
        + _page_specs(n_pages, layer, LANES) + _page_specs(n_pages, layer, LANES),
        out_specs=pl.BlockSpec((1, t, D_ATT), lambda bi, pt: (bi, 0, 0)),
    )
    return pl.pallas_call(
        _dsa_sample_attn_kernel,
        grid_spec=grid_spec,
        out_shape=jax.ShapeDtypeStruct((b, t, D_ATT), F32),
        compiler_params=_cp(1),
        name="dsa_sample_attn",
    )(page_table, z, z, z, keys, thr, cut, bias_s, *([cache_k] * n_pages), *([cache_v] * n_pages))


def _merge_kernel(ya_ref, yb_ref, yc_ref, ga_ref, gb_ref, gc_ref, x_ref, g1_ref, sh2_ref, sc2_ref,
                  wpa_ref, wpb_ref, wpc_ref, wo_ref, lng_ref, lnb_ref, wrh_ref, wrl_ref, rb_ref,
                  x1_ref, h2_ref, gate_ref, *, alpha):
    gbk, rbk, d = x_ref.shape
    m = gbk * rbk

    def flat(ref):
        return ref[...].reshape(m, ref.shape[-1])

    merged = (_sigmoid(flat(ga_ref)) * _dot(flat(ya_ref).astype(BF16), wpa_ref[...])
              + _sigmoid(flat(gb_ref)) * _dot(flat(yb_ref).astype(BF16), wpb_ref[...])
              + _sigmoid(flat(gc_ref)) * _dot(flat(yc_ref).astype(BF16), wpc_ref[...]))
    mix = _dot(merged.astype(BF16), wo_ref[...]).reshape(gbk, rbk, d)
    x1 = _ln(alpha * x_ref[...] + (1.0 + g1_ref[...]) * mix, lng_ref[...], lnb_ref[...])
    x1_ref[...] = x1
    h2 = x1 * (1.0 + sc2_ref[...]) + sh2_ref[...]
    h2_ref[...] = h2.astype(h2_ref.dtype)

    h2f = h2.reshape(m, d)
    hi = h2f.astype(BF16)
    lo = (h2f - hi.astype(F32)).astype(BF16)
    logits = _dot_nt(wrh_ref[...], hi) + _dot_nt(wrl_ref[...], hi) + _dot_nt(wrh_ref[...], lo)
    scores = _sigmoid(logits)
    sel = scores + rb_ref[...]
    row = lax.broadcasted_iota(I32, (N_EXPERTS, m), 0).astype(F32)
    best = jnp.zeros((1, m), F32)
    best_score = None
    for g in range(N_GROUPS):
        r = [sel[g * EXPERTS_PER_GROUP + i:g * EXPERTS_PER_GROUP + i + 1] for i in range(EXPERTS_PER_GROUP)]
        top2 = None
        for i in range(EXPERTS_PER_GROUP):
            for j in range(i + 1, EXPERTS_PER_GROUP):
                pair = r[i] + r[j]
                top2 = pair if top2 is None else jnp.maximum(top2, pair)
        if best_score is None:
            best_score = top2
        else:
            better = top2 > best_score
            best = jnp.where(better, float(g), best)
            best_score = jnp.where(better, top2, best_score)
    lo_row = best * EXPERTS_PER_GROUP
    in_group = (row >= lo_row) & (row < lo_row + EXPERTS_PER_GROUP)
    masked = jnp.where(in_group, sel, -jnp.inf)
    m1 = jnp.max(masked, axis=0, keepdims=True)
    i1 = jnp.min(jnp.where(masked == m1, row, float(N_EXPERTS)), axis=0, keepdims=True)
    masked2 = jnp.where(row == i1, -jnp.inf, masked)
    m2 = jnp.max(masked2, axis=0, keepdims=True)
    i2 = jnp.min(jnp.where(masked2 == m2, row, float(N_EXPERTS)), axis=0, keepdims=True)
    w1 = jnp.sum(jnp.where(row == i1, scores, 0.0), axis=0, keepdims=True)
    w2 = jnp.sum(jnp.where(row == i2, scores, 0.0), axis=0, keepdims=True)
    tot = w1 + w2
    gate_ref[...] = jnp.where(row == i1, w1 / tot, 0.0) + jnp.where(row == i2, w2 / tot, 0.0)


def _merge(ya, yb, yc, z, x3, mod, w_pa, w_pb, w_pc, w_o, lng, lnb, wr_hi, wr_lo, rbias, gb, rb, act_dtype, alpha):
    g, r, d = x3.shape
    rt = r // rb

    def act(width, col):
        return pl.BlockSpec((gb, rb, width), lambda gi, ri: (gi, ri, col))

    def modspec(col):
        return pl.BlockSpec((gb, 1, d), lambda gi, ri: (gi, 0, col))

    def full(a):
        return pl.BlockSpec(a.shape, lambda gi, ri: (0,) * a.ndim)

    return pl.pallas_call(
        functools.partial(_merge_kernel, alpha=alpha),
        grid=(g // gb, rt),
        in_specs=[act(D_A, 0), act(D_ATT, 0), act(D_C, 0),
                  act(d, C_GA // d), act(d, C_GB // d), act(d, C_GC // d),
                  act(d, 0), modspec(2), modspec(3), modspec(4),
                  full(w_pa), full(w_pb), full(w_pc), full(w_o), full(lng), full(lnb),
                  full(wr_hi), full(wr_lo), full(rbias)],
        out_specs=[act(d, 0), act(d, 0),
                   pl.BlockSpec((N_EXPERTS, gb * rb), lambda gi, ri: (0, gi * rt + ri))],
        out_shape=[jax.ShapeDtypeStruct((g, r, d), F32),
                   jax.ShapeDtypeStruct((g, r, d), act_dtype),
                   jax.ShapeDtypeStruct((N_EXPERTS, g * r), F32)],
        compiler_params=_cp(2),
        name="merge",
    )(ya, yb, yc, z, z, z, x3, mod, mod, mod, w_pa, w_pb, w_pc, w_o, lng, lnb, wr_hi, wr_lo, rbias)


def _moe_kernel(h_ref, gate_ref, x1_ref, g2_ref, wg_ref, wu_ref, wd_ref, lng_ref, lnb_ref, o_ref, acc_scr, *, alpha):
    e = pl.program_id(2)
    gbk, rbk, d = h_ref.shape
    m = gbk * rbk

    @pl.when(e == 0)
    def _():
        acc_scr[...] = jnp.zeros(acc_scr.shape, F32)

    h = h_ref[...].reshape(m, d).astype(BF16)
    a = _dot(h, wg_ref[0, 0])
    u = _dot(h, wu_ref[0, 0])
    he = (a * _sigmoid(a)) * u
    gate = gate_ref[...]
    lane = lax.broadcasted_iota(I32, gate.shape, 1)
    gcol = jnp.sum(jnp.where(lane == e, gate, 0.0), axis=1, keepdims=True)
    acc_scr[...] += gcol * _dot(he.astype(BF16), wd_ref[0, 0])

    @pl.when(e == pl.num_programs(2) - 1)
    def _():
        y = acc_scr[...].reshape(gbk, rbk, d)
        o_ref[...] = _ln(alpha * x1_ref[...] + (1.0 + g2_ref[...]) * y, lng_ref[...], lnb_ref[...])


def _moe(h2, gate, x1, mod, w_g, w_u, w_d, lng, lnb, layer, gb, rb, alpha):
    g, r, d = x1.shape
    rt = r // rb
    n_e = w_g.shape[1]
    return pl.pallas_call(
        functools.partial(_moe_kernel, alpha=alpha),
        grid=(g // gb, rt, n_e),
        in_specs=[pl.BlockSpec((gb, rb, d), lambda gi, ri, e: (gi, ri, 0)),
                  pl.BlockSpec((gb * rb, n_e), lambda gi, ri, e: (gi * rt + ri, 0)),
                  pl.BlockSpec((gb, rb, d), lambda gi, ri, e: (gi, ri, 0)),
                  pl.BlockSpec((gb, 1, d), lambda gi, ri, e: (gi, 0, 5)),
                  pl.BlockSpec((1, 1, d, D_EXPERT), lambda gi, ri, e: (layer, e, 0, 0)),
                  pl.BlockSpec((1, 1, d, D_EXPERT), lambda gi, ri, e: (layer, e, 0, 0)),
                  pl.BlockSpec((1, 1, D_EXPERT, d), lambda gi, ri, e: (layer, e, 0, 0)),
                  pl.BlockSpec((1, d), lambda gi, ri, e: (0, 0)),
                  pl.BlockSpec((1, d), lambda gi, ri, e: (0, 0))],
        out_specs=pl.BlockSpec((gb, rb, d), lambda gi, ri, e: (gi, ri, 0)),
        out_shape=jax.ShapeDtypeStruct((g, r, d), F32),
        scratch_shapes=[pltpu.VMEM((gb * rb, d), F32)],
        compiler_params=_cp(3),
        name="moe",
    )(h2, gate, x1, mod, w_g, w_u, w_d, lng, lnb)


def _pad_cols(n, dtype, like):
    return jnp.zeros(like.shape[:-1] + (n,), dtype)


def _relayout_in(w):
    def sl(o, n):
        return w[..., o:o + n]
    ki = sl(_O_KI, D_IDX)
    parts = [sl(_O_AGLU, 2 * D_A), sl(_O_UV, 2 * D_C), sl(_O_G, 3 * D_MODEL), sl(_O_Q, D_ATT),
             sl(_O_QI, N_IDX_HEADS * D_IDX), sl(_O_K, N_KV * HEAD_DIM), sl(_O_V, N_KV * HEAD_DIM),
             ki, ki, ki, ki, sl(_O_WI, N_IDX_HEADS), _pad_cols(D_ZP - C_WI - N_IDX_HEADS, w.dtype, w)]
    return jnp.concatenate(parts, axis=-1)


def kernel(x_prompt, x_sample, cache_k, cache_v, cache_kidx, state_conv, page_table, c_prompt, c_sample,
           w_ada, b_ada, w_in, b_in, conv_k, conv_b, lnA_g, lnA_b, lnC_g, lnC_b, w_s, b_s,
           w_pa, w_pb, w_pc, w_o, ln1_g, ln1_b, ln2_g, ln2_b, rel_table, w_router, router_bias,
           w_e_gate, w_e_up, w_e_down):
    depth = w_in.shape[0]
    alpha = (2 * depth) ** 0.25
    bp, tp, d = x_prompt.shape
    bs, ts, _ = x_sample.shape
    n_pages = page_table.shape[1]
    past = n_pages * PAGE_SIZE
    topk_s = min(TOPK_MAX, (past + ts) // 4)
    n_pool = cache_k.shape[1]

    w_in_p = _relayout_in(w_in).astype(BF16)
    b_in_p = _relayout_in(b_in).reshape(depth, 1, D_ZP)
    w_pa_b, w_pb_b, w_pc_b, w_o_b = (w.astype(BF16) for w in (w_pa, w_pb, w_pc, w_o))
    w_g_b, w_u_b, w_d_b = (w.astype(BF16) for w in (w_e_gate, w_e_up, w_e_down))
    wr_t = w_router.T
    wr_hi = wr_t.astype(BF16)
    wr_lo = (wr_t - wr_hi.astype(F32)).astype(BF16)
    rbias = router_bias.reshape(N_EXPERTS, 1)
    grp_w = D_C // N_CGROUPS
    bs_tab_p = jnp.repeat(jnp.swapaxes(b_s[:, :, :CHUNK], 1, 2), grp_w, axis=2)
    bs_tab_s = bs_tab_p[:, :ts]
    wl_tab = jnp.repeat(jnp.transpose(w_s[:, :, :ts, :ts], (0, 3, 2, 1)), grp_w, axis=3)
    cache_k2 = cache_k.reshape(depth, n_pool, PAGE_SIZE, N_KV * HEAD_DIM)
    cache_v2 = cache_v.reshape(depth, n_pool, PAGE_SIZE, N_KV * HEAD_DIM)

    def row(a, l):
        return a[l].reshape(1, -1)

    bias_p = _bias_near(rel_table)
    bias_s = bias_p.reshape(N_KV, GROUP, KEY_BLK, 2 * KEY_BLK)[:, :, :ts].reshape(N_KV, GROUP * ts, 2 * KEY_BLK)

    n_c = bp + bs
    c_all = jnp.concatenate([c_prompt, c_sample, jnp.zeros((-n_c % 8, d), F32)], axis=0)
    mod_all = _ada(c_all, w_ada, b_ada)

    yp, ys = x_prompt, x_sample
    outs = {k: [] for k in ("kp", "vp", "kip", "cp", "ks", "vs", "kis", "cs", "vcs")}
    rb_p = 512
    for l in range(depth):
        mod_p = mod_all[l, :bp].reshape(bp, 1, 6 * d)
        mod_s = mod_all[l, bp:n_c].reshape(bs, 1, 6 * d)
        lnA = (row(lnA_g, l), row(lnA_b, l))
        lnC = (row(lnC_g, l), row(lnC_b, l))
        ln1 = (row(ln1_g, l), row(ln1_b, l))
        ln2 = (row(ln2_g, l), row(ln2_b, l))

        z = _inproj(yp, mod_p, w_in_p[l], b_in_p[l], 1, rb_p, 3328)
        ya, conv_state = _conv_prompt(z, conv_k[l], row(conv_b, l), *lnA)
        yb = _dsa_prompt(z, bias_p)
        yc = _sgu_prompt(z, *lnC, w_s[l], bs_tab_p[l])
        x1, h2, gate_t = _merge(ya, yb, yc, z, yp, mod_p, w_pa_b[l], w_pb_b[l], w_pc_b[l], w_o_b[l], *ln1,
                                wr_hi, wr_lo, rbias, 1, rb_p, BF16, alpha)
        yp = _moe(h2, gate_t.T, x1, mod_p, w_g_b, w_u_b, w_d_b, *ln2, l, 1, 1024, alpha)
        outs["kp"].append(z[:, :, C_K:C_K + N_KV * HEAD_DIM].reshape(bp, tp, N_KV, HEAD_DIM))
        outs["vp"].append(z[:, :, C_VV:C_VV + N_KV * HEAD_DIM].reshape(bp, tp, N_KV, HEAD_DIM))
        outs["kip"].append(z[:, :, C_KI4:C_KI4 + D_IDX])
        outs["cp"].append(conv_state)

        zs = _inproj(ys, mod_s, w_in_p[l], b_in_p[l], bs, ts, 1664)
        ya, conv_state = _conv_sample(zs, state_conv[l], conv_k[l], row(conv_b, l), *lnA)
        keys = _dsa_sample_scores(zs, cache_kidx, page_table, l)
        thr, cut = _dsa_sample_topk(keys, topk_s)
        yb = _dsa_sample_attn(zs, keys, thr, cut, cache_k2, cache_v2, page_table, l, bias_s)
        yc, v_chunk = _sgu_sample(zs, *lnC, wl_tab[l], bs_tab_s[l])
        x1, h2, gate_t = _merge(ya, yb, yc, zs, ys, mod_s, w_pa_b[l], w_pb_b[l], w_pc_b[l], w_o_b[l], *ln1,
                                wr_hi, wr_lo, rbias, bs, ts, F32, alpha)
        ys = _moe(h2, gate_t.T, x1, mod_s, w_g_b, w_u_b, w_d_b, *ln2, l, bs, ts, alpha)
        outs["ks"].append(zs[:, :, C_K:C_K + N_KV * HEAD_DIM].reshape(bs, ts, N_KV, HEAD_DIM))
        outs["vs"].append(zs[:, :, C_VV:C_VV + N_KV * HEAD_DIM].reshape(bs, ts, N_KV, HEAD_DIM))
        outs["kis"].append(zs[:, :, C_KI4:C_KI4 + D_IDX])
        outs["cs"].append(conv_state)
        outs["vcs"].append(v_chunk)

    st = {k: jnp.stack(v) for k, v in outs.items()}
    return (yp, ys, st["kp"], st["vp"], st["kip"], st["cp"],
            st["ks"], st["vs"], st["kis"], st["cs"], st["vcs"])
```

```python
import functools
import math

import jax
import jax.numpy as jnp
from jax import lax
from jax.experimental import pallas as pl
from jax.experimental.pallas import tpu as pltpu

F32 = jnp.float32
BF16 = jnp.bfloat16
I32 = jnp.int32

D_MODEL = 1024
D_A = 512
CONV_W = 31
N_HEADS = 8
N_KV = 2
GROUP = N_HEADS // N_KV
HEAD_DIM = 64
D_ATT = N_HEADS * HEAD_DIM
N_IDX_HEADS = 4
D_IDX = 64
TOPK_MAX = 256
N_BUCKETS = 32
MAX_DIST = 128
D_C = 512
N_CGROUPS = 8
CHUNK = 128
N_EXPERTS = 16
N_GROUPS = 4
EXPERTS_PER_GROUP = N_EXPERTS // N_GROUPS
D_EXPERT = 512
PAGE_SIZE = 128
LN_EPS = 1e-5

LANES = 128
KEY_BLK = 128
FAR_BLKS = 4
NEG_BIG = -1e30
LOG2E = math.log2(math.e)
INT_MIN = -2 ** 31
VMEM_LIMIT = 56 * 1024 * 1024

C_A, C_AG, C_U, C_V = 0, 512, 1024, 1536
C_GA, C_GB, C_GC = 2048, 3072, 4096
C_Q, C_QI, C_K, C_VV, C_KI4, C_WI = 5120, 5632, 5888, 6016, 6144, 6400
D_ZP = 6656

_O_AGLU, _O_Q, _O_K, _O_V, _O_QI, _O_KI, _O_WI, _O_UV, _O_G = 0, 1024, 1536, 1664, 1792, 2048, 2112, 2116, 3140
_D_IN = 6212


def _cp(n_axes):
    return pltpu.CompilerParams(dimension_semantics=("arbitrary",) * n_axes, vmem_limit_bytes=VMEM_LIMIT)


def _sigmoid(x):
    return jax.nn.sigmoid(x)


def _ln(x, g, b):
    mu = jnp.mean(x, axis=-1, keepdims=True)
    xc = x - mu
    var = jnp.mean(xc * xc, axis=-1, keepdims=True)
    return xc * lax.rsqrt(var + LN_EPS) * g + b


def _dot(a, b):
    return jnp.dot(a, b, preferred_element_type=F32)


def _dot_nt(a, b):
    return lax.dot_general(a, b, (((1,), (1,)), ((), ())), preferred_element_type=F32)


def _ada_kernel(c_ref, w_ref, b_ref, o_ref):
    c = c_ref[...]
    s = (c * _sigmoid(c)).astype(BF16)
    o_ref[0] = _dot(s, w_ref[0].astype(BF16)) + b_ref[0]


def _ada(c_all, w_ada, b_ada):
    depth, d, n = w_ada.shape
    m = c_all.shape[0]
    tn = 1536
    return pl.pallas_call(
        _ada_kernel,
        grid=(depth, n // tn),
        in_specs=[pl.BlockSpec((m, d), lambda l, j: (0, 0)),
                  pl.BlockSpec((1, d, tn), lambda l, j: (l, 0, j)),
                  pl.BlockSpec((1, 1, tn), lambda l, j: (l, 0, j))],
        out_specs=pl.BlockSpec((1, m, tn), lambda l, j: (l, 0, j)),
        out_shape=jax.ShapeDtypeStruct((depth, m, n), F32),
        compiler_params=_cp(2),
        name="ada",
    )(c_all, w_ada, b_ada.reshape(depth, 1, n))


def _inproj_kernel(x_ref, sh_ref, sc_ref, w_ref, b_ref, z_ref):
    gb, rb, d = x_ref.shape
    h = x_ref[...] * (1.0 + sc_ref[...]) + sh_ref[...]
    z = _dot(h.reshape(gb * rb, d).astype(BF16), w_ref[...]) + b_ref[...]
    z_ref[...] = z.reshape(gb, rb, z.shape[-1])


def _inproj(x3, mod, w, b, gb, rb, tn):
    g, r, d = x3.shape
    n = w.shape[1]
    return pl.pallas_call(
        _inproj_kernel,
        grid=(n // tn, g // gb, r // rb),
        in_specs=[pl.BlockSpec((gb, rb, d), lambda j, gi, ri: (gi, ri, 0)),
                  pl.BlockSpec((gb, 1, d), lambda j, gi, ri: (gi, 0, 0)),
                  pl.BlockSpec((gb, 1, d), lambda j, gi, ri: (gi, 0, 1)),
                  pl.BlockSpec((d, tn), lambda j, gi, ri: (0, j)),
                  pl.BlockSpec((1, tn), lambda j, gi, ri: (0, j))],
        out_specs=pl.BlockSpec((gb, rb, tn), lambda j, gi, ri: (gi, ri, j)),
        out_shape=jax.ShapeDtypeStruct((g, r, n), F32),
        compiler_params=_cp(3),
        name="inproj",
    )(x3, mod, mod, w, b)


CONV_HALO = 32
CONV_ROWS = 32


def _conv_prompt_kernel(a_ref, g_ref, ah_ref, gh_ref, ck_ref, cb_ref, lg_ref, lb_ref, y_ref, st_ref, ext_scr):
    t = pl.program_id(1)
    tt = a_ref.shape[1]
    a = a_ref[0] * _sigmoid(g_ref[0])
    halo = ah_ref[0] * _sigmoid(gh_ref[0])
    ext_scr[0:CONV_HALO, :] = jnp.where(t > 0, halo, 0.0)
    ext_scr[CONV_HALO:, :] = a
    off = CONV_HALO - (CONV_W - 1)
    for c in range(tt // CONV_ROWS):
        r0 = c * CONV_ROWS
        acc = jnp.zeros((CONV_ROWS, D_A), F32) + cb_ref[...]
        for j in range(CONV_W):
            acc = acc + ck_ref[j:j + 1, :] * ext_scr[r0 + off + j:r0 + off + j + CONV_ROWS, :]
        y = _ln(acc, lg_ref[...], lb_ref[...])
        y_ref[0, r0:r0 + CONV_ROWS, :] = (y * _sigmoid(y)).astype(y_ref.dtype)

    @pl.when(t == pl.num_programs(1) - 1)
    def _():
        st_ref[0] = ext_scr[tt + off:tt + CONV_HALO, :]


def _conv_prompt(z, ck, cb, lg, lb, tt=256):
    b, t, _ = z.shape
    hb = tt // CONV_HALO
    return pl.pallas_call(
        _conv_prompt_kernel,
        grid=(b, t // tt),
        in_specs=[pl.BlockSpec((1, tt, D_A), lambda bi, ti: (bi, ti, C_A // D_A)),
                  pl.BlockSpec((1, tt, D_A), lambda bi, ti: (bi, ti, C_AG // D_A)),
                  pl.BlockSpec((1, CONV_HALO, D_A), lambda bi, ti: (bi, jnp.maximum(ti * hb - 1, 0), C_A // D_A)),
                  pl.BlockSpec((1, CONV_HALO, D_A), lambda bi, ti: (bi, jnp.maximum(ti * hb - 1, 0), C_AG // D_A)),
                  pl.BlockSpec((CONV_W, D_A), lambda bi, ti: (0, 0)),
                  pl.BlockSpec((1, D_A), lambda bi, ti: (0, 0)),
                  pl.BlockSpec((1, D_A), lambda bi, ti: (0, 0)),
                  pl.BlockSpec((1, D_A), lambda bi, ti: (0, 0))],
        out_specs=[pl.BlockSpec((1, tt, D_A), lambda bi, ti: (bi, ti, 0)),
                   pl.BlockSpec((1, CONV_W - 1, D_A), lambda bi, ti: (bi, 0, 0))],
        out_shape=[jax.ShapeDtypeStruct((b, t, D_A), BF16),
                   jax.ShapeDtypeStruct((b, CONV_W - 1, D_A), F32)],
        scratch_shapes=[pltpu.VMEM((tt + CONV_HALO, D_A), F32)],
        compiler_params=_cp(2),
        name="conv_prompt",
    )(z, z, z, z, ck, cb, lg, lb)


def _conv_sample_kernel(a_ref, g_ref, st_ref, ck_ref, cb_ref, lg_ref, lb_ref, y_ref, ns_ref, ext_scr):
    bs, t, _ = a_ref.shape
    hist = CONV_W - 1
    ext_scr[:, 0:hist, :] = st_ref[...]
    ext_scr[:, hist:hist + t, :] = a_ref[...] * _sigmoid(g_ref[...])
    ns_ref[...] = ext_scr[:, t:t + hist, :]

    def body(b, carry):
        acc = jnp.zeros((t, D_A), F32) + cb_ref[...]
        for j in range(CONV_W):
            acc = acc + ck_ref[j:j + 1, :] * ext_scr[b, j:j + t, :]
        y = _ln(acc, lg_ref[...], lb_ref[...])
        y_ref[b] = y * _sigmoid(y)
        return carry

    lax.fori_loop(0, bs, body, 0)


def _conv_sample(z, state, ck, cb, lg, lb, bs=32):
    b, t, _ = z.shape
    hist = CONV_W - 1
    bs = min(bs, b)
    return pl.pallas_call(
        _conv_sample_kernel,
        grid=(b // bs,),
        in_specs=[pl.BlockSpec((bs, t, D_A), lambda bi: (bi, 0, C_A // D_A)),
                  pl.BlockSpec((bs, t, D_A), lambda bi: (bi, 0, C_AG // D_A)),
                  pl.BlockSpec((bs, hist, D_A), lambda bi: (bi, 0, 0)),
                  pl.BlockSpec((CONV_W, D_A), lambda bi: (0, 0)),
                  pl.BlockSpec((1, D_A), lambda bi: (0, 0)),
                  pl.BlockSpec((1, D_A), lambda bi: (0, 0)),
                  pl.BlockSpec((1, D_A), lambda bi: (0, 0))],
        out_specs=[pl.BlockSpec((bs, t, D_A), lambda bi: (bi, 0, 0)),
                   pl.BlockSpec((bs, hist, D_A), lambda bi: (bi, 0, 0))],
        out_shape=[jax.ShapeDtypeStruct((b, t, D_A), F32),
                   jax.ShapeDtypeStruct((b, hist, D_A), F32)],
        scratch_shapes=[pltpu.VMEM((bs, hist + t + 2, D_A), F32)],
        compiler_params=_cp(1),
        name="conv_sample",
    )(z, z, state, ck, cb, lg, lb)


def _sgu_prompt_kernel(u_ref, v_ref, lg_ref, lb_ref, ws_ref, bs_ref, y_ref):
    tt = u_ref.shape[1]
    vv = _ln(v_ref[0], lg_ref[...], lb_ref[...])
    row = lax.broadcasted_iota(I32, (CHUNK, CHUNK), 0)
    col = lax.broadcasted_iota(I32, (CHUNK, CHUNK), 1)
    lane_grp = lax.broadcasted_iota(I32, (CHUNK, D_C), 1) // (D_C // N_CGROUPS)
    ws = [jnp.where(col <= row, ws_ref[g], 0.0).astype(BF16) for g in range(N_CGROUPS)]
    for c in range(tt // CHUNK):
        vc = vv[c * CHUNK:(c + 1) * CHUNK].astype(BF16)
        mixed = bs_ref[...]
        for g in range(N_CGROUPS):
            mixed = mixed + _dot(ws[g], jnp.where(lane_grp == g, vc, jnp.zeros_like(vc)))
        y_ref[0, c * CHUNK:(c + 1) * CHUNK, :] = (u_ref[0, c * CHUNK:(c + 1) * CHUNK, :] * mixed).astype(y_ref.dtype)


def _sgu_prompt(z, lg, lb, w_s, bs_tab, tt=512):
    b, t, _ = z.shape
    return pl.pallas_call(
        _sgu_prompt_kernel,
        grid=(b, t // tt),
        in_specs=[pl.BlockSpec((1, tt, D_C), lambda bi, ti: (bi, ti, C_U // D_C)),
                  pl.BlockSpec((1, tt, D_C), lambda bi, ti: (bi, ti, C_V // D_C)),
                  pl.BlockSpec((1, D_C), lambda bi, ti: (0, 0)),
                  pl.BlockSpec((1, D_C), lambda bi, ti: (0, 0)),
                  pl.BlockSpec((N_CGROUPS, CHUNK, CHUNK), lambda bi, ti: (0, 0, 0)),
                  pl.BlockSpec((CHUNK, D_C), lambda bi, ti: (0, 0))],
        out_specs=pl.BlockSpec((1, tt, D_C), lambda bi, ti: (bi, ti, 0)),
        out_shape=jax.ShapeDtypeStruct((b, t, D_C), BF16),
        compiler_params=_cp(2),
        name="sgu_prompt",
    )(z, z, lg, lb, w_s, bs_tab)


def _sgu_sample_kernel(u_ref, v_ref, lg_ref, lb_ref, wl_ref, bs_ref, y_ref, vc_ref):
    bs, t, _ = u_ref.shape
    vv = _ln(v_ref[...], lg_ref[...], lb_ref[...])
    vc_ref[...] = vv
    tpos = lax.broadcasted_iota(I32, (t, D_C), 0)
    mixed = jnp.zeros((bs, t, D_C), F32) + bs_ref[...]
    for s in range(t):
        w = jnp.where(tpos >= s, wl_ref[s], 0.0)
        mixed = mixed + w * vv[:, s:s + 1, :]
    y_ref[...] = u_ref[...] * mixed


def _sgu_sample(z, lg, lb, wl_tab, bs_tab):
    b, t, _ = z.shape
    return pl.pallas_call(
        _sgu_sample_kernel,
        grid=(1,),
        in_specs=[pl.BlockSpec((b, t, D_C), lambda i: (0, 0, C_U // D_C)),
                  pl.BlockSpec((b, t, D_C), lambda i: (0, 0, C_V // D_C)),
                  pl.BlockSpec((1, D_C), lambda i: (0, 0)),
                  pl.BlockSpec((1, D_C), lambda i: (0, 0)),
                  pl.BlockSpec((t, t, D_C), lambda i: (0, 0, 0)),
                  pl.BlockSpec((t, D_C), lambda i: (0, 0))],
        out_specs=[pl.BlockSpec((b, t, D_C), lambda i: (0, 0, 0)),
                   pl.BlockSpec((b, t, D_C), lambda i: (0, 0, 0))],
        out_shape=[jax.ShapeDtypeStruct((b, t, D_C), F32),
                   jax.ShapeDtypeStruct((b, t, D_C), F32)],
        compiler_params=_cp(1),
        name="sgu_sample",
    )(z, z, lg, lb, wl_tab, bs_tab)


def _bias_kernel(tab_ref, o_ref):
    rows, cols = KEY_BLK, 2 * KEY_BLK
    i = lax.broadcasted_iota(I32, (rows, cols), 0)
    j = lax.broadcasted_iota(I32, (rows, cols), 1)
    n = jnp.maximum(KEY_BLK + i - j, 0)
    max_exact = N_BUCKETS // 2
    nf = jnp.maximum(n, 1).astype(F32)
    large = max_exact + (jnp.log(nf / max_exact) / math.log(MAX_DIST / max_exact) * (N_BUCKETS - max_exact)).astype(I32)
    large = jnp.minimum(large, N_BUCKETS - 1)
    bucket = jnp.where(n < max_exact, n, large)
    for h in range(N_HEADS):
        acc = jnp.zeros((rows, cols), F32)
        for bkt in range(N_BUCKETS):
            acc = jnp.where(bucket == bkt, tab_ref[bkt, h], acc)
        o_ref[h // GROUP, (h % GROUP) * rows:(h % GROUP + 1) * rows, :] = (acc - tab_ref[N_BUCKETS - 1, h]) * LOG2E


def _bias_near(rel_table):
    return pl.pallas_call(
        _bias_kernel,
        grid=(1,),
        in_specs=[pl.BlockSpec(memory_space=pltpu.SMEM)],
        out_specs=pl.BlockSpec((N_KV, GROUP * KEY_BLK, 2 * KEY_BLK), lambda i: (0, 0, 0)),
        out_shape=jax.ShapeDtypeStruct((N_KV, GROUP * KEY_BLK, 2 * KEY_BLK), F32),
        compiler_params=_cp(1),
        name="bias_near",
    )(rel_table)


def _score_keys(s):
    s = jnp.where(s == 0.0, 0.0, s)
    bits = lax.bitcast_convert_type(s, I32)
    return bits ^ (jnp.right_shift(bits, 31) & 0x7FFFFFFF)


def _lane_top2(s, m1, m2):
    for blk in [s[:, c * LANES:(c + 1) * LANES] for c in range(s.shape[1] // LANES)]:
        m2 = jnp.maximum(m2, jnp.minimum(m1, blk))
        m1 = jnp.maximum(m1, blk)
    return m1, m2


def _topk_threshold(load_blk, n_iter, unroll, rows, k, idx_bits, cut_scr, bounds=None):
    lane = lax.broadcasted_iota(I32, (rows, KEY_BLK), 1)
    if bounds is None:
        first_bit = 0
        thr0 = jnp.full((rows, 1), INT_MIN, I32)
    else:
        lo, hi = bounds
        first_bit = jnp.minimum(jnp.min(lax.clz(lo ^ hi)), 31)
        keep = ~(jnp.left_shift(jnp.int32(2), 31 - first_bit) - 1)
        thr0 = ((hi ^ INT_MIN) & keep) ^ INT_MIN

    def count(pred):
        def body(jj, acc):
            for u in range(unroll):
                j = jj * unroll + u
                acc = acc + jnp.where(pred(load_blk(j), j), 1.0, 0.0)
            return acc
        acc = lax.fori_loop(0, n_iter, body, jnp.zeros((rows, KEY_BLK), F32))
        return jnp.sum(acc, axis=1, keepdims=True)

    kf = float(k)

    def bit_body(i, thr):
        cand = thr ^ jnp.left_shift(jnp.int32(1), 31 - i)
        cnt = count(lambda kb, j: kb >= cand)
        return jnp.where(cnt >= kf, cand, thr)

    thr = lax.fori_loop(first_bit, 32, bit_body, thr0)
    cnt_gt = count(lambda kb, j: kb > thr)
    need = kf - cnt_gt
    if cut_scr is None:
        return thr, need
    cnt_ge = count(lambda kb, j: kb >= thr)
    cut_scr[...] = jnp.full((rows, 1), 2 ** idx_bits, I32)

    @pl.when(jnp.max(cnt_ge) > kf)
    def _():
        def idx_body(i, r):
            cand = r + jnp.left_shift(jnp.int32(1), idx_bits - 1 - i)
            cnt = count(lambda kb, j: (kb == thr) & (j * KEY_BLK + lane < cand))
            return jnp.where(cnt < need, cand, r)
        r = lax.fori_loop(0, idx_bits, idx_body, jnp.zeros((rows, 1), I32))
        cut_scr[...] = r + 1

    return thr, cut_scr[...]


def _select_mask(keys, thr, cut, gidx):
    return (keys > thr) | ((keys == thr) & (gidx < cut))


def _stack_heads(q, n, rows):
    half = lax.broadcasted_iota(I32, (rows, LANES), 1) // HEAD_DIM
    tiles = []
    for g in range(GROUP):
        h = n * GROUP + g
        col = q[:, (h // 2) * LANES:(h // 2 + 1) * LANES]
        if h % 2 != n:
            col = pltpu.roll(col, HEAD_DIM, 1)
        tiles.append(jnp.where(half == n, col, 0.0))
    return jnp.concatenate(tiles, axis=0)


def _unstack_heads(o_groups, rows):
    half = lax.broadcasted_iota(I32, (rows, LANES), 1) // HEAD_DIM
    cols = []
    for c in range(N_HEADS // 2):
        parts = []
        for h in (2 * c, 2 * c + 1):
            n, g = h // GROUP, h % GROUP
            tile = o_groups[n][g * rows:(g + 1) * rows, :]
            if h % 2 != n:
                tile = pltpu.roll(tile, HEAD_DIM, 1)
            parts.append(tile)
        cols.append(jnp.where(half == 0, parts[0], parts[1]))
    return jnp.concatenate(cols, axis=1)


def _dsa_prompt_kernel(q_ref, qi_ref, wi_ref, ki_ref, k_ref, v_ref, bias_ref, o_ref,
                       key_scr, qs_scr, mpart_scr, mfull_scr, lpart_scr, acc_scr, p_scr,
                       lgfar_scr, lgnear_scr, *, topk):
    qb = pl.program_id(1)
    rows = KEY_BLK
    chunk = FAR_BLKS * KEY_BLK
    row_i = lax.broadcasted_iota(I32, (rows, 1), 0)
    tpos = qb * rows + row_i

    qi = qi_ref[0]
    lane_head = lax.broadcasted_iota(I32, qi.shape, 1) // D_IDX
    qim = [jnp.where(lane_head == h, qi, 0.0).astype(BF16) for h in range(N_IDX_HEADS)]
    wi = wi_ref[0]
    lane_c = lax.broadcasted_iota(I32, (rows, chunk), 1)
    n_chunks = qb // FAR_BLKS + 1

    def score_body(c, top2):
        kc = ki_ref[0, pl.ds(pl.multiple_of(c * chunk, chunk), chunk), :].astype(BF16)
        s = jnp.zeros((rows, chunk), F32)
        for h in range(N_IDX_HEADS):
            s = s + wi[:, h:h + 1] * jnp.maximum(_dot_nt(qim[h], kc), 0.0)
        s = jnp.where(c * chunk + lane_c <= tpos, s, -jnp.inf)
        keys = _score_keys(s)
        for j in range(FAR_BLKS):
            key_scr[c * FAR_BLKS + j] = keys[:, j * KEY_BLK:(j + 1) * KEY_BLK]
        return _lane_top2(s, *top2)

    ninf = jnp.full((rows, LANES), -jnp.inf, F32)
    m1, m2 = lax.fori_loop(0, n_chunks, score_body, (ninf, ninf))

    assert topk <= 2 * LANES
    bounds = (_score_keys(jnp.min(m2, axis=1, keepdims=True)), _score_keys(jnp.max(m1, axis=1, keepdims=True)))
    thr, need = _topk_threshold(lambda j: key_scr[j], n_chunks, FAR_BLKS, rows, topk, 0, None, bounds)

    ri = lax.broadcasted_iota(I32, (KEY_BLK, KEY_BLK), 0)
    ci = lax.broadcasted_iota(I32, (KEY_BLK, KEY_BLK), 1)
    prefix_mat = jnp.where(ri <= ci, 1.0, 0.0).astype(BF16)
    ones_mat = jnp.ones((KEY_BLK, KEY_BLK), BF16)

    def select_block(kb, run):
        eq = kb == thr
        e = jnp.where(eq, 1.0, 0.0).astype(BF16)
        sel = (kb > thr) | (eq & (run + _dot(e, prefix_mat) <= need))
        return sel, run + _dot(e, ones_mat)

    q = q_ref[0] * (HEAD_DIM ** -0.5 * LOG2E)
    for n in range(N_KV):
        qs_scr[n] = _stack_heads(q, n, rows).astype(BF16)
    mpart_scr[...] = jnp.full(mpart_scr.shape, NEG_BIG, F32)
    lpart_scr[...] = jnp.zeros(lpart_scr.shape, F32)
    acc_scr[...] = jnp.zeros(acc_scr.shape, F32)

    def lane_blocks(t):
        return [t[:, c * LANES:(c + 1) * LANES] for c in range(t.shape[1] // LANES)]

    def fold(vals, op):
        while len(vals) > 1:
            vals = [op(vals[i], vals[i + 1]) for i in range(0, len(vals), 2)]
        return vals[0]

    def sweep_max(kc, neg, with_bias, lg_ref):
        for n in range(N_KV):
            lg = _dot_nt(qs_scr[n], kc)
            for g in range(GROUP):
                sl = slice(g * rows, (g + 1) * rows)
                t = lg[sl] + neg
                if with_bias:
                    t = t + bias_ref[n, sl, :]
                lg_ref[n, sl, :] = t
                mpart_scr[n, sl, :] = jnp.maximum(mpart_scr[n, sl, :], fold(lane_blocks(t), jnp.maximum))

    def sweep_acc(vc, lg_ref):
        width = vc.shape[0]
        for n in range(N_KV):
            for g in range(GROUP):
                sl = slice(g * rows, (g + 1) * rows)
                mf = mfull_scr[n, sl, :]
                ps = [jnp.exp2(tb - mf) for tb in lane_blocks(lg_ref[n, sl, :])]
                lpart_scr[n, sl, :] = lpart_scr[n, sl, :] + fold(ps, jnp.add)
                p_scr[sl, 0:width] = jnp.concatenate(ps, axis=1).astype(BF16)
            acc_scr[n] = acc_scr[n] + _dot(p_scr[:, 0:width], vc)

    n_far = (qb + 2) // FAR_BLKS

    def far_max(c, run):
        start = pl.multiple_of(c * chunk, chunk)
        kc = k_ref[0, pl.ds(start, chunk), :].astype(BF16)
        negs = []
        for j in range(FAR_BLKS):
            blk = c * FAR_BLKS + j
            is_far = blk < qb - 1
            sel, run_next = select_block(key_scr[blk], run)
            run = jnp.where(is_far, run_next, run)
            negs.append(jnp.where(sel & is_far, 0.0, NEG_BIG))
        sweep_max(kc, jnp.concatenate(negs, axis=1), False, lgfar_scr.at[c])
        return run

    run = lax.fori_loop(0, n_far, far_max, jnp.zeros((rows, KEY_BLK), F32))

    pb = jnp.maximum(qb - 1, 0)
    p0 = pl.multiple_of(pb * rows, rows)
    q0 = pl.multiple_of(qb * rows, rows)
    lane_b = lax.broadcasted_iota(I32, (rows, KEY_BLK), 1)
    sel_prev, run_next = select_block(key_scr[pb], run)
    run = jnp.where(qb > 0, run_next, run)
    sel_prev = sel_prev & (qb > 0)
    sel_diag, _ = select_block(key_scr[qb], run)
    sel_diag = sel_diag & (qb * rows + lane_b <= tpos)
    neg_near = jnp.concatenate([jnp.where(sel_prev, 0.0, NEG_BIG), jnp.where(sel_diag, 0.0, NEG_BIG)], axis=1)

    def near_kv(ref):
        return jnp.concatenate([ref[0, pl.ds(p0, rows), :], ref[0, pl.ds(q0, rows), :]], axis=0).astype(BF16)

    sweep_max(near_kv(k_ref), neg_near, True, lgnear_scr)
    for n in range(N_KV):
        mfull_scr[n] = jnp.broadcast_to(jnp.max(mpart_scr[n], axis=1, keepdims=True), mfull_scr.shape[1:])

    def far_acc(c, carry):
        start = pl.multiple_of(c * chunk, chunk)
        sweep_acc(v_ref[0, pl.ds(start, chunk), :].astype(BF16), lgfar_scr.at[c])
        return carry

    lax.fori_loop(0, n_far, far_acc, 0)
    sweep_acc(near_kv(v_ref), lgnear_scr)

    outs = [acc_scr[n] / jnp.sum(lpart_scr[n], axis=1, keepdims=True) for n in range(N_KV)]
    o_ref[0] = _unstack_heads(outs, rows).astype(o_ref.dtype)


def _dsa_prompt(z, bias_near):
    b, t, _ = z.shape
    nb = t // KEY_BLK
    assert nb % FAR_BLKS == 0
    topk = min(TOPK_MAX, t // 4)
    grows = GROUP * KEY_BLK
    chunk = FAR_BLKS * KEY_BLK
    return pl.pallas_call(
        functools.partial(_dsa_prompt_kernel, topk=topk),
        grid=(b, nb),
        in_specs=[pl.BlockSpec((1, KEY_BLK, D_ATT), lambda bi, qi: (bi, qi, C_Q // D_ATT)),
                  pl.BlockSpec((1, KEY_BLK, 256), lambda bi, qi: (bi, qi, C_QI // 256)),
                  pl.BlockSpec((1, KEY_BLK, LANES), lambda bi, qi: (bi, qi, C_WI // LANES)),
                  pl.BlockSpec((1, t, 256), lambda bi, qi: (bi, 0, C_KI4 // 256)),
                  pl.BlockSpec((1, t, LANES), lambda bi, qi: (bi, 0, C_K // LANES)),
                  pl.BlockSpec((1, t, LANES), lambda bi, qi: (bi, 0, C_VV // LANES)),
                  pl.BlockSpec((N_KV, grows, 2 * KEY_BLK), lambda bi, qi: (0, 0, 0))],
        out_specs=pl.BlockSpec((1, KEY_BLK, D_ATT), lambda bi, qi: (bi, qi, 0)),
        out_shape=jax.ShapeDtypeStruct((b, t, D_ATT), BF16),
        scratch_shapes=[pltpu.VMEM((nb, KEY_BLK, KEY_BLK), I32),
                        pltpu.VMEM((N_KV, grows, LANES), BF16),
                        pltpu.VMEM((N_KV, grows, LANES), F32),
                        pltpu.VMEM((N_KV, grows, LANES), F32),
                        pltpu.VMEM((N_KV, grows, LANES), F32),
                        pltpu.VMEM((N_KV, grows, LANES), F32),
                        pltpu.VMEM((grows, chunk), BF16),
                        pltpu.VMEM((nb // FAR_BLKS, N_KV, grows, chunk), F32),
                        pltpu.VMEM((N_KV, grows, 2 * KEY_BLK), F32)],
        compiler_params=_cp(2),
        name="dsa_prompt",
    )(z, z, z, z, z, z, bias_near)


def _dsa_sample_score_kernel(pt_ref, qi_ref, wi_ref, kin_ref, *rest):
    n_pages = len(rest) - 1
    pages, o_ref = rest[:n_pages], rest[n_pages]
    t = qi_ref.shape[1]
    qi = qi_ref[0]
    wi = wi_ref[0]
    qs = jnp.concatenate([qi[:, h * D_IDX:(h + 1) * D_IDX] for h in range(N_IDX_HEADS)], axis=0).astype(BF16)
    k_past_t = jnp.concatenate([pg[0, 0] for pg in pages], axis=1).astype(BF16)
    k_new = jnp.concatenate([kin_ref[0][:, :D_IDX], jnp.zeros((KEY_BLK - t, D_IDX), F32)], axis=0).astype(BF16)

    def score(d):
        s = jnp.zeros((t, d.shape[1]), F32)
        for h in range(N_IDX_HEADS):
            s = s + wi[:, h:h + 1] * jnp.maximum(d[h * t:(h + 1) * t], 0.0)
        return s

    keys_past = _score_keys(score(_dot(qs, k_past_t)))
    for j in range(n_pages):
        o_ref[j] = keys_past[:, j * KEY_BLK:(j + 1) * KEY_BLK]
    s_new = score(_dot_nt(qs, k_new))
    lane = lax.broadcasted_iota(I32, s_new.shape, 1)
    trow = lax.broadcasted_iota(I32, s_new.shape, 0)
    o_ref[n_pages] = _score_keys(jnp.where(lane <= trow, s_new, -jnp.inf))


def _page_specs(n_pages, layer, feat):
    return [pl.BlockSpec((1, 1, feat, PAGE_SIZE), lambda bi, pt, j=j: (layer, pt[bi, j], 0, 0))
            for j in range(n_pages)]


def _dsa_sample_scores(z, cache_kidx, page_table, layer):
    b, t, _ = z.shape
    n_pages = page_table.shape[1]
    nblk = n_pages + 1
    grid_spec = pltpu.PrefetchScalarGridSpec(
        num_scalar_prefetch=1,
        grid=(b,),
        in_specs=[pl.BlockSpec((1, t, 256), lambda bi, pt: (bi, 0, C_QI // 256)),
                  pl.BlockSpec((1, t, LANES), lambda bi, pt: (bi, 0, C_WI // LANES)),
                  pl.BlockSpec((1, t, 256), lambda bi, pt: (bi, 0, C_KI4 // 256))]
        + _page_specs(n_pages, layer, D_IDX),
        out_specs=pl.BlockSpec((nblk, t, KEY_BLK), lambda bi, pt: (0, bi, 0)),
    )
    return pl.pallas_call(
        _dsa_sample_score_kernel,
        grid_spec=grid_spec,
        out_shape=jax.ShapeDtypeStruct((nblk, b * t, KEY_BLK), I32),
        compiler_params=_cp(1),
        name="dsa_sample_scores",
    )(page_table, z, z, z, *([cache_kidx] * n_pages))


def _dsa_sample_topk_kernel(key_ref, thr_ref, cut_ref, cut_scr, *, topk):
    nblk, rows, _ = key_ref.shape
    thr, cut = _topk_threshold(lambda j: key_ref[j], 1, nblk, rows, topk, 13, cut_scr)
    thr_ref[...] = thr
    cut_ref[...] = cut


def _dsa_sample_topk(keys, topk, rows=128):
    nblk, m, _ = keys.shape
    return pl.pallas_call(
        functools.partial(_dsa_sample_topk_kernel, topk=topk),
        grid=(m // rows,),
        in_specs=[pl.BlockSpec((nblk, rows, KEY_BLK), lambda i: (0, i, 0))],
        out_specs=[pl.BlockSpec((rows, 1), lambda i: (i, 0)),
                   pl.BlockSpec((rows, 1), lambda i: (i, 0))],
        out_shape=[jax.ShapeDtypeStruct((m, 1), I32), jax.ShapeDtypeStruct((m, 1), I32)],
        scratch_shapes=[pltpu.VMEM((rows, 1), I32)],
        compiler_params=_cp(1),
        name="dsa_sample_topk",
    )(keys)


def _dsa_sample_attn_kernel(pt_ref, q_ref, kn_ref, vn_ref, key_ref, thr_ref, cut_ref, bias_ref, *rest):
    n_pages = (len(rest) - 1) // 2
    kpages, vpages, o_ref = rest[:n_pages], rest[n_pages:2 * n_pages], rest[2 * n_pages]
    t = q_ref.shape[1]
    past = n_pages * PAGE_SIZE
    thr = thr_ref[...]
    cut = cut_ref[...]
    q = q_ref[0] * (HEAD_DIM ** -0.5 * LOG2E)
    trow = lax.broadcasted_iota(I32, (t, 1), 0)

    n_far = n_pages - 1
    kt_far = jnp.concatenate([pg[0, 0] for pg in kpages[:n_far]], axis=1).astype(BF16)
    vt_far = jnp.concatenate([pg[0, 0] for pg in vpages[:n_far]], axis=1).astype(BF16)
    keys_far = jnp.concatenate([key_ref[j] for j in range(n_far)], axis=1)
    g_far = lax.broadcasted_iota(I32, keys_far.shape, 1)
    neg_far = jnp.where(_select_mask(keys_far, thr, cut, g_far), 0.0, NEG_BIG)

    pad = jnp.zeros((KEY_BLK - t, LANES), F32)
    kt_last = kpages[n_far][0, 0].astype(BF16)
    vt_last = vpages[n_far][0, 0].astype(BF16)
    k_new = jnp.concatenate([kn_ref[0], pad], axis=0).astype(BF16)
    v_new = jnp.concatenate([vn_ref[0], pad], axis=0).astype(BF16)
    lane_b = lax.broadcasted_iota(I32, (t, KEY_BLK), 1)
    sel_last = _select_mask(key_ref[n_far], thr, cut, n_far * KEY_BLK + lane_b)
    sel_new = _select_mask(key_ref[n_pages], thr, cut, past + lane_b) & (lane_b <= trow)
    neg_near = jnp.concatenate([jnp.where(sel_last, 0.0, NEG_BIG), jnp.where(sel_new, 0.0, NEG_BIG)], axis=1)

    outs = []
    for n in range(N_KV):
        qs = _stack_heads(q, n, t).astype(BF16)
        lg_far = _dot(qs, kt_far) + jnp.concatenate([neg_far] * GROUP, axis=0)
        lg_near = (jnp.concatenate([_dot(qs, kt_last), _dot_nt(qs, k_new)], axis=1) + bias_ref[n]
                   + jnp.concatenate([neg_near] * GROUP, axis=0))
        m = jnp.maximum(jnp.max(lg_far, axis=1, keepdims=True), jnp.max(lg_near, axis=1, keepdims=True))
        p_far = jnp.exp2(lg_far - m)
        p_near = jnp.exp2(lg_near - m)
        l = jnp.sum(p_far, axis=1, keepdims=True) + jnp.sum(p_near, axis=1, keepdims=True)
        p_near = p_near.astype(BF16)
        o = (_dot_nt(p_far.astype(BF16), vt_far) + _dot_nt(p_near[:, :KEY_BLK], vt_last)
             + _dot(p_near[:, KEY_BLK:], v_new))
        outs.append(o / l)
    o_ref[0] = _unstack_heads(outs, t)


def _dsa_sample_attn(z, keys, thr, cut, cache_k, cache_v, page_table, layer, bias_s):
    b, t, _ = z.shape
    n_pages = page_table.shape[1]
    nblk = n_pages + 1
    grid_spec = pltpu.PrefetchScalarGridSpec(
        num_scalar_prefetch=1,
        grid=(b,),
        in_specs=[pl.BlockSpec((1, t, D_ATT), lambda bi, pt: (bi, 0, C_Q // D_ATT)),
                  pl.BlockSpec((1, t, LANES), lambda bi, pt: (bi, 0, C_K // LANES)),
                  pl.BlockSpec((1, t, LANES), lambda bi, pt: (bi, 0, C_VV // LANES)),
                  pl.BlockSpec((nblk, t, KEY_BLK), lambda bi, pt: (0, bi, 0)),
                  pl.BlockSpec((t, 1), lambda bi, pt: (bi, 0)),
                  pl.BlockSpec((t, 1), lambda bi, pt: (bi, 0)),
                  pl.BlockSpec((N_KV, GROUP * t, 2 * KEY_BLK), lambda bi, pt: (0, 0, 0))]
        + _page_specs(n_pages, layer, LANES) + _page_specs(n_pages, layer, LANES),
        out_specs=pl.BlockSpec((1, t, D_ATT), lambda bi, pt: (bi, 0, 0)),
    )
    return pl.pallas_call(
        _dsa_sample_attn_kernel,
        grid_spec=grid_spec,
        out_shape=jax.ShapeDtypeStruct((b, t, D_ATT), F32),
        compiler_params=_cp(1),
        name="dsa_sample_attn",
    )(page_table, z, z, z, keys, thr, cut, bias_s, *([cache_k] * n_pages), *([cache_v] * n_pages))


def _merge_kernel(ya_ref, yb_ref, yc_ref, ga_ref, gb_ref, gc_ref, x_ref, g1_ref, sh2_ref, sc2_ref,
                  wpa_ref, wpb_ref, wpc_ref, wo_ref, lng_ref, lnb_ref, wrh_ref, wrl_ref, rb_ref,
                  x1_ref, h2_ref, gate_ref, *, alpha):
    gbk, rbk, d = x_ref.shape
    m = gbk * rbk

    def flat(ref):
        return ref[...].reshape(m, ref.shape[-1])

    merged = (_sigmoid(flat(ga_ref)) * _dot(flat(ya_ref).astype(BF16), wpa_ref[...])
              + _sigmoid(flat(gb_ref)) * _dot(flat(yb_ref).astype(BF16), wpb_ref[...])
              + _sigmoid(flat(gc_ref)) * _dot(flat(yc_ref).astype(BF16), wpc_ref[...]))
    mix = _dot(merged.astype(BF16), wo_ref[...]).reshape(gbk, rbk, d)
    x1 = _ln(alpha * x_ref[...] + (1.0 + g1_ref[...]) * mix, lng_ref[...], lnb_ref[...])
    x1_ref[...] = x1
    h2 = x1 * (1.0 + sc2_ref[...]) + sh2_ref[...]
    h2_ref[...] = h2.astype(h2_ref.dtype)

    h2f = h2.reshape(m, d)
    hi = h2f.astype(BF16)
    lo = (h2f - hi.astype(F32)).astype(BF16)
    logits = _dot_nt(wrh_ref[...], hi) + _dot_nt(wrl_ref[...], hi) + _dot_nt(wrh_ref[...], lo)
    scores = _sigmoid(logits)
    sel = scores + rb_ref[...]
    row = lax.broadcasted_iota(I32, (N_EXPERTS, m), 0).astype(F32)
    best = jnp.zeros((1, m), F32)
    best_score = None
    for g in range(N_GROUPS):
        r = [sel[g * EXPERTS_PER_GROUP + i:g * EXPERTS_PER_GROUP + i + 1] for i in range(EXPERTS_PER_GROUP)]
        top2 = None
        for i in range(EXPERTS_PER_GROUP):
            for j in range(i + 1, EXPERTS_PER_GROUP):
                pair = r[i] + r[j]
                top2 = pair if top2 is None else jnp.maximum(top2, pair)
        if best_score is None:
            best_score = top2
        else:
            better = top2 > best_score
            best = jnp.where(better, float(g), best)
            best_score = jnp.where(better, top2, best_score)
    lo_row = best * EXPERTS_PER_GROUP
    in_group = (row >= lo_row) & (row < lo_row + EXPERTS_PER_GROUP)
    masked = jnp.where(in_group, sel, -jnp.inf)
    m1 = jnp.max(masked, axis=0, keepdims=True)
    i1 = jnp.min(jnp.where(masked == m1, row, float(N_EXPERTS)), axis=0, keepdims=True)
    masked2 = jnp.where(row == i1, -jnp.inf, masked)
    m2 = jnp.max(masked2, axis=0, keepdims=True)
    i2 = jnp.min(jnp.where(masked2 == m2, row, float(N_EXPERTS)), axis=0, keepdims=True)
    w1 = jnp.sum(jnp.where(row == i1, scores, 0.0), axis=0, keepdims=True)
    w2 = jnp.sum(jnp.where(row == i2, scores, 0.0), axis=0, keepdims=True)
    tot = w1 + w2
    gate_ref[...] = jnp.where(row == i1, w1 / tot, 0.0) + jnp.where(row == i2, w2 / tot, 0.0)


def _merge(ya, yb, yc, z, x3, mod, w_pa, w_pb, w_pc, w_o, lng, lnb, wr_hi, wr_lo, rbias, gb, rb, act_dtype, alpha):
    g, r, d = x3.shape
    rt = r // rb

    def act(width, col):
        return pl.BlockSpec((gb, rb, width), lambda gi, ri: (gi, ri, col))

    def modspec(col):
        return pl.BlockSpec((gb, 1, d), lambda gi, ri: (gi, 0, col))

    def full(a):
        return pl.BlockSpec(a.shape, lambda gi, ri: (0,) * a.ndim)

    return pl.pallas_call(
        functools.partial(_merge_kernel, alpha=alpha),
        grid=(g // gb, rt),
        in_specs=[act(D_A, 0), act(D_ATT, 0), act(D_C, 0),
                  act(d, C_GA // d), act(d, C_GB // d), act(d, C_GC // d),
                  act(d, 0), modspec(2), modspec(3), modspec(4),
                  full(w_pa), full(w_pb), full(w_pc), full(w_o), full(lng), full(lnb),
                  full(wr_hi), full(wr_lo), full(rbias)],
        out_specs=[act(d, 0), act(d, 0),
                   pl.BlockSpec((N_EXPERTS, gb * rb), lambda gi, ri: (0, gi * rt + ri))],
        out_shape=[jax.ShapeDtypeStruct((g, r, d), F32),
                   jax.ShapeDtypeStruct((g, r, d), act_dtype),
                   jax.ShapeDtypeStruct((N_EXPERTS, g * r), F32)],
        compiler_params=_cp(2),
        name="merge",
    )(ya, yb, yc, z, z, z, x3, mod, mod, mod, w_pa, w_pb, w_pc, w_o, lng, lnb, wr_hi, wr_lo, rbias)


def _moe_kernel(h_ref, gate_ref, x1_ref, g2_ref, wg_ref, wu_ref, wd_ref, lng_ref, lnb_ref, o_ref, acc_scr, *, alpha):
    e = pl.program_id(2)
    gbk, rbk, d = h_ref.shape
    m = gbk * rbk

    @pl.when(e == 0)
    def _():
        acc_scr[...] = jnp.zeros(acc_scr.shape, F32)

    h = h_ref[...].reshape(m, d).astype(BF16)
    a = _dot(h, wg_ref[0, 0])
    u = _dot(h, wu_ref[0, 0])
    he = (a * _sigmoid(a)) * u
    gate = gate_ref[...]
    lane = lax.broadcasted_iota(I32, gate.shape, 1)
    gcol = jnp.sum(jnp.where(lane == e, gate, 0.0), axis=1, keepdims=True)
    acc_scr[...] += gcol * _dot(he.astype(BF16), wd_ref[0, 0])

    @pl.when(e == pl.num_programs(2) - 1)
    def _():
        y = acc_scr[...].reshape(gbk, rbk, d)
        o_ref[...] = _ln(alpha * x1_ref[...] + (1.0 + g2_ref[...]) * y, lng_ref[...], lnb_ref[...])


def _moe(h2, gate, x1, mod, w_g, w_u, w_d, lng, lnb, layer, gb, rb, alpha):
    g, r, d = x1.shape
    rt = r // rb
    n_e = w_g.shape[1]
    return pl.pallas_call(
        functools.partial(_moe_kernel, alpha=alpha),
        grid=(g // gb, rt, n_e),
        in_specs=[pl.BlockSpec((gb, rb, d), lambda gi, ri, e: (gi, ri, 0)),
                  pl.BlockSpec((gb * rb, n_e), lambda gi, ri, e: (gi * rt + ri, 0)),
                  pl.BlockSpec((gb, rb, d), lambda gi, ri, e: (gi, ri, 0)),
                  pl.BlockSpec((gb, 1, d), lambda gi, ri, e: (gi, 0, 5)),
                  pl.BlockSpec((1, 1, d, D_EXPERT), lambda gi, ri, e: (layer, e, 0, 0)),
                  pl.BlockSpec((1, 1, d, D_EXPERT), lambda gi, ri, e: (layer, e, 0, 0)),
                  pl.BlockSpec((1, 1, D_EXPERT, d), lambda gi, ri, e: (layer, e, 0, 0)),
                  pl.BlockSpec((1, d), lambda gi, ri, e: (0, 0)),
                  pl.BlockSpec((1, d), lambda gi, ri, e: (0, 0))],
        out_specs=pl.BlockSpec((gb, rb, d), lambda gi, ri, e: (gi, ri, 0)),
        out_shape=jax.ShapeDtypeStruct((g, r, d), F32),
        scratch_shapes=[pltpu.VMEM((gb * rb, d), F32)],
        compiler_params=_cp(3),
        name="moe",
    )(h2, gate, x1, mod, w_g, w_u, w_d, lng, lnb)


def _pad_cols(n, dtype, like):
    return jnp.zeros(like.shape[:-1] + (n,), dtype)


def _relayout_in(w):
    def sl(o, n):
        return w[..., o:o + n]
    ki = sl(_O_KI, D_IDX)
    parts = [sl(_O_AGLU, 2 * D_A), sl(_O_UV, 2 * D_C), sl(_O_G, 3 * D_MODEL), sl(_O_Q, D_ATT),
             sl(_O_QI, N_IDX_HEADS * D_IDX), sl(_O_K, N_KV * HEAD_DIM), sl(_O_V, N_KV * HEAD_DIM),
             ki, ki, ki, ki, sl(_O_WI, N_IDX_HEADS), _pad_cols(D_ZP - C_WI - N_IDX_HEADS, w.dtype, w)]
    return jnp.concatenate(parts, axis=-1)


def kernel(x_prompt, x_sample, cache_k, cache_v, cache_kidx, state_conv, page_table, c_prompt, c_sample,
           w_ada, b_ada, w_in, b_in, conv_k, conv_b, lnA_g, lnA_b, lnC_g, lnC_b, w_s, b_s,
           w_pa, w_pb, w_pc, w_o, ln1_g, ln1_b, ln2_g, ln2_b, rel_table, w_router, router_bias,
           w_e_gate, w_e_up, w_e_down):
    depth = w_in.shape[0]
    alpha = (2 * depth) ** 0.25
    bp, tp, d = x_prompt.shape
    bs, ts, _ = x_sample.shape
    n_pages = page_table.shape[1]
    past = n_pages * PAGE_SIZE
    topk_s = min(TOPK_MAX, (past + ts) // 4)
    n_pool = cache_k.shape[1]

    w_in_p = _relayout_in(w_in).astype(BF16)
    b_in_p = _relayout_in(b_in).reshape(depth, 1, D_ZP)
    w_pa_b, w_pb_b, w_pc_b, w_o_b = (w.astype(BF16) for w in (w_pa, w_pb, w_pc, w_o))
    w_g_b, w_u_b, w_d_b = (w.astype(BF16) for w in (w_e_gate, w_e_up, w_e_down))
    wr_t = w_router.T
    wr_hi = wr_t.astype(BF16)
    wr_lo = (wr_t - wr_hi.astype(F32)).astype(BF16)
    rbias = router_bias.reshape(N_EXPERTS, 1)
    grp_w = D_C // N_CGROUPS
    bs_tab_p = jnp.repeat(jnp.swapaxes(b_s[:, :, :CHUNK], 1, 2), grp_w, axis=2)
    bs_tab_s = bs_tab_p[:, :ts]
    wl_tab = jnp.repeat(jnp.transpose(w_s[:, :, :ts, :ts], (0, 3, 2, 1)), grp_w, axis=3)
    cache_k2 = jnp.transpose(cache_k, (0, 1, 3, 4, 2)).reshape(depth, n_pool, N_KV * HEAD_DIM, PAGE_SIZE)
    cache_v2 = jnp.transpose(cache_v, (0, 1, 3, 4, 2)).reshape(depth, n_pool, N_KV * HEAD_DIM, PAGE_SIZE)
    cache_ki2 = jnp.transpose(cache_kidx, (0, 1, 3, 2))

    def row(a, l):
        return a[l].reshape(1, -1)

    bias_p = _bias_near(rel_table)
    bias_s = bias_p.reshape(N_KV, GROUP, KEY_BLK, 2 * KEY_BLK)[:, :, :ts].reshape(N_KV, GROUP * ts, 2 * KEY_BLK)

    n_c = bp + bs
    c_all = jnp.concatenate([c_prompt, c_sample, jnp.zeros((-n_c % 8, d), F32)], axis=0)
    mod_all = _ada(c_all, w_ada, b_ada)

    yp, ys = x_prompt, x_sample
    outs = {k: [] for k in ("kp", "vp", "kip", "cp", "ks", "vs", "kis", "cs", "vcs")}
    rb_p = 512
    for l in range(depth):
        mod_p = mod_all[l, :bp].reshape(bp, 1, 6 * d)
        mod_s = mod_all[l, bp:n_c].reshape(bs, 1, 6 * d)
        lnA = (row(lnA_g, l), row(lnA_b, l))
        lnC = (row(lnC_g, l), row(lnC_b, l))
        ln1 = (row(ln1_g, l), row(ln1_b, l))
        ln2 = (row(ln2_g, l), row(ln2_b, l))

        z = _inproj(yp, mod_p, w_in_p[l], b_in_p[l], 1, rb_p, 3328)
        ya, conv_state = _conv_prompt(z, conv_k[l], row(conv_b, l), *lnA)
        yb = _dsa_prompt(z, bias_p)
        yc = _sgu_prompt(z, *lnC, w_s[l], bs_tab_p[l])
        x1, h2, gate_t = _merge(ya, yb, yc, z, yp, mod_p, w_pa_b[l], w_pb_b[l], w_pc_b[l], w_o_b[l], *ln1,
                                wr_hi, wr_lo, rbias, 1, rb_p, BF16, alpha)
        yp = _moe(h2, gate_t.T, x1, mod_p, w_g_b, w_u_b, w_d_b, *ln2, l, 1, 1024, alpha)
        outs["kp"].append(z[:, :, C_K:C_K + N_KV * HEAD_DIM].reshape(bp, tp, N_KV, HEAD_DIM))
        outs["vp"].append(z[:, :, C_VV:C_VV + N_KV * HEAD_DIM].reshape(bp, tp, N_KV, HEAD_DIM))
        outs["kip"].append(z[:, :, C_KI4:C_KI4 + D_IDX])
        outs["cp"].append(conv_state)

        zs = _inproj(ys, mod_s, w_in_p[l], b_in_p[l], bs, ts, 1664)
        ya, conv_state = _conv_sample(zs, state_conv[l], conv_k[l], row(conv_b, l), *lnA)
        keys = _dsa_sample_scores(zs, cache_ki2, page_table, l)
        thr, cut = _dsa_sample_topk(keys, topk_s)
        yb = _dsa_sample_attn(zs, keys, thr, cut, cache_k2, cache_v2, page_table, l, bias_s)
        yc, v_chunk = _sgu_sample(zs, *lnC, wl_tab[l], bs_tab_s[l])
        x1, h2, gate_t = _merge(ya, yb, yc, zs, ys, mod_s, w_pa_b[l], w_pb_b[l], w_pc_b[l], w_o_b[l], *ln1,
                                wr_hi, wr_lo, rbias, bs, ts, F32, alpha)
        ys = _moe(h2, gate_t.T, x1, mod_s, w_g_b, w_u_b, w_d_b, *ln2, l, bs, ts, alpha)
        outs["ks"].append(zs[:, :, C_K:C_K + N_KV * HEAD_DIM].reshape(bs, ts, N_KV, HEAD_DIM))
        outs["vs"].append(zs[:, :, C_VV:C_VV + N_KV * HEAD_DIM].reshape(bs, ts, N_KV, HEAD_DIM))
        outs["kis"].append(zs[:, :, C_KI4:C_KI4 + D_IDX])
        outs["cs"].append(conv_state)
        outs["vcs"].append(v_chunk)

    st = {k: jnp.stack(v) for k, v in outs.items()}
    return (yp, ys, st["kp"], st["vp"], st["kip"], st["cp"],
            st["ks"], st["vs"], st["kis"], st["cs"], st["vcs"])
```

```python
import functools
import math

import jax
import jax.numpy as jnp
from jax import lax
from jax.experimental import pallas as pl
from jax.experimental.pallas import tpu as pltpu

F32 = jnp.float32
BF16 = jnp.bfloat16
I32 = jnp.int32

D_MODEL = 1024
D_A = 512
CONV_W = 31
N_HEADS = 8
N_KV = 2
GROUP = N_HEADS // N_KV
HEAD_DIM = 64
D_ATT = N_HEADS * HEAD_DIM
N_IDX_HEADS = 4
D_IDX = 64
TOPK_MAX = 256
N_BUCKETS = 32
MAX_DIST = 128
D_C = 512
N_CGROUPS = 8
CHUNK = 128
N_EXPERTS = 16
N_GROUPS = 4
EXPERTS_PER_GROUP = N_EXPERTS // N_GROUPS
D_EXPERT = 512
PAGE_SIZE = 128
LN_EPS = 1e-5

LANES = 128
KEY_BLK = 128
FAR_BLKS = 4
Q_PAIR = 2
NEG_BIG = -1e30
LOG2E = math.log2(math.e)
INT_MIN = -2 ** 31
VMEM_LIMIT = 56 * 1024 * 1024

C_A, C_AG, C_U, C_V = 0, 512, 1024, 1536
C_GA, C_GB, C_GC = 2048, 3072, 4096
C_Q, C_QI, C_K, C_VV, C_KI4, C_WI = 5120, 5632, 5888, 6016, 6144, 6400
D_ZP = 6656

_O_AGLU, _O_Q, _O_K, _O_V, _O_QI, _O_KI, _O_WI, _O_UV, _O_G = 0, 1024, 1536, 1664, 1792, 2048, 2112, 2116, 3140
_D_IN = 6212


def _cp(n_axes):
    return pltpu.CompilerParams(dimension_semantics=("arbitrary",) * n_axes, vmem_limit_bytes=VMEM_LIMIT)


def _sigmoid(x):
    return jax.nn.sigmoid(x)


def _ln(x, g, b):
    mu = jnp.mean(x, axis=-1, keepdims=True)
    xc = x - mu
    var = jnp.mean(xc * xc, axis=-1, keepdims=True)
    return xc * lax.rsqrt(var + LN_EPS) * g + b


def _dot(a, b):
    return jnp.dot(a, b, preferred_element_type=F32)


def _dot_nt(a, b):
    return lax.dot_general(a, b, (((1,), (1,)), ((), ())), preferred_element_type=F32)


def _ada_kernel(c_ref, w_ref, b_ref, o_ref):
    c = c_ref[...]
    s = (c * _sigmoid(c)).astype(BF16)
    o_ref[0] = _dot(s, w_ref[0].astype(BF16)) + b_ref[0]


def _ada(c_all, w_ada, b_ada):
    depth, d, n = w_ada.shape
    m = c_all.shape[0]
    tn = 1536
    return pl.pallas_call(
        _ada_kernel,
        grid=(depth, n // tn),
        in_specs=[pl.BlockSpec((m, d), lambda l, j: (0, 0)),
                  pl.BlockSpec((1, d, tn), lambda l, j: (l, 0, j)),
                  pl.BlockSpec((1, 1, tn), lambda l, j: (l, 0, j))],
        out_specs=pl.BlockSpec((1, m, tn), lambda l, j: (l, 0, j)),
        out_shape=jax.ShapeDtypeStruct((depth, m, n), F32),
        compiler_params=_cp(2),
        name="ada",
    )(c_all, w_ada, b_ada.reshape(depth, 1, n))


def _inproj_kernel(x_ref, sh_ref, sc_ref, w_ref, b_ref, z_ref):
    gb, rb, d = x_ref.shape
    h = x_ref[...] * (1.0 + sc_ref[...]) + sh_ref[...]
    z = _dot(h.reshape(gb * rb, d).astype(BF16), w_ref[...]) + b_ref[...]
    z_ref[...] = z.reshape(gb, rb, z.shape[-1])


def _inproj(x3, mod, w, b, gb, rb, tn):
    g, r, d = x3.shape
    n = w.shape[1]
    return pl.pallas_call(
        _inproj_kernel,
        grid=(n // tn, g // gb, r // rb),
        in_specs=[pl.BlockSpec((gb, rb, d), lambda j, gi, ri: (gi, ri, 0)),
                  pl.BlockSpec((gb, 1, d), lambda j, gi, ri: (gi, 0, 0)),
                  pl.BlockSpec((gb, 1, d), lambda j, gi, ri: (gi, 0, 1)),
                  pl.BlockSpec((d, tn), lambda j, gi, ri: (0, j)),
                  pl.BlockSpec((1, tn), lambda j, gi, ri: (0, j))],
        out_specs=pl.BlockSpec((gb, rb, tn), lambda j, gi, ri: (gi, ri, j)),
        out_shape=jax.ShapeDtypeStruct((g, r, n), F32),
        compiler_params=_cp(3),
        name="inproj",
    )(x3, mod, mod, w, b)


CONV_HALO = 32
CONV_ROWS = 32


SUBLANES = 8


def _conv_prompt_kernel(a_ref, g_ref, ah_ref, gh_ref, ck_ref, cb_ref, lg_ref, lb_ref, y_ref, st_ref,
                        ext_scr, sh_scr):
    t = pl.program_id(1)
    tt = a_ref.shape[1]
    a = a_ref[0] * _sigmoid(g_ref[0])
    halo = ah_ref[0] * _sigmoid(gh_ref[0])
    ext_scr[0:CONV_HALO, :] = jnp.where(t > 0, halo, 0.0)
    ext_scr[CONV_HALO:, :] = a
    off = CONV_HALO - (CONV_W - 1)
    ext_rows = tt + CONV_HALO
    for ph in range(SUBLANES):
        n_rows = ext_rows if ph == 0 else ext_rows - SUBLANES
        sh_scr[ph, 0:n_rows, :] = ext_scr[ph:ph + n_rows, :]
    for c in range(tt // CONV_ROWS):
        r0 = c * CONV_ROWS
        acc = jnp.zeros((CONV_ROWS, D_A), F32) + cb_ref[...]
        for j in range(CONV_W):
            ph, base = (off + j) % SUBLANES, (off + j) // SUBLANES * SUBLANES
            acc = acc + ck_ref[j:j + 1, :] * sh_scr[ph, r0 + base:r0 + base + CONV_ROWS, :]
        y = _ln(acc, lg_ref[...], lb_ref[...])
        y_ref[0, r0:r0 + CONV_ROWS, :] = (y * _sigmoid(y)).astype(y_ref.dtype)

    @pl.when(t == pl.num_programs(1) - 1)
    def _():
        st_ref[0] = ext_scr[tt + off:tt + CONV_HALO, :]


def _conv_prompt(z, ck, cb, lg, lb, tt=256):
    b, t, _ = z.shape
    hb = tt // CONV_HALO
    return pl.pallas_call(
        _conv_prompt_kernel,
        grid=(b, t // tt),
        in_specs=[pl.BlockSpec((1, tt, D_A), lambda bi, ti: (bi, ti, C_A // D_A)),
                  pl.BlockSpec((1, tt, D_A), lambda bi, ti: (bi, ti, C_AG // D_A)),
                  pl.BlockSpec((1, CONV_HALO, D_A), lambda bi, ti: (bi, jnp.maximum(ti * hb - 1, 0), C_A // D_A)),
                  pl.BlockSpec((1, CONV_HALO, D_A), lambda bi, ti: (bi, jnp.maximum(ti * hb - 1, 0), C_AG // D_A)),
                  pl.BlockSpec((CONV_W, D_A), lambda bi, ti: (0, 0)),
                  pl.BlockSpec((1, D_A), lambda bi, ti: (0, 0)),
                  pl.BlockSpec((1, D_A), lambda bi, ti: (0, 0)),
                  pl.BlockSpec((1, D_A), lambda bi, ti: (0, 0))],
        out_specs=[pl.BlockSpec((1, tt, D_A), lambda bi, ti: (bi, ti, 0)),
                   pl.BlockSpec((1, CONV_W - 1, D_A), lambda bi, ti: (bi, 0, 0))],
        out_shape=[jax.ShapeDtypeStruct((b, t, D_A), BF16),
                   jax.ShapeDtypeStruct((b, CONV_W - 1, D_A), F32)],
        scratch_shapes=[pltpu.VMEM((tt + CONV_HALO, D_A), F32),
                        pltpu.VMEM((SUBLANES, tt + CONV_HALO, D_A), F32)],
        compiler_params=_cp(2),
        name="conv_prompt",
    )(z, z, z, z, ck, cb, lg, lb)


def _conv_sample_kernel(a_ref, g_ref, st_ref, ck_ref, cb_ref, lg_ref, lb_ref, y_ref, ns_ref, ext_scr):
    bs, t, _ = a_ref.shape
    hist = CONV_W - 1
    ext_scr[:, 0:hist, :] = st_ref[...]
    ext_scr[:, hist:hist + t, :] = a_ref[...] * _sigmoid(g_ref[...])
    ns_ref[...] = ext_scr[:, t:t + hist, :]

    def body(b, carry):
        acc = jnp.zeros((t, D_A), F32) + cb_ref[...]
        for j in range(CONV_W):
            acc = acc + ck_ref[j:j + 1, :] * ext_scr[b, j:j + t, :]
        y = _ln(acc, lg_ref[...], lb_ref[...])
        y_ref[b] = y * _sigmoid(y)
        return carry

    lax.fori_loop(0, bs, body, 0)


def _conv_sample(z, state, ck, cb, lg, lb, bs=32):
    b, t, _ = z.shape
    hist = CONV_W - 1
    bs = min(bs, b)
    return pl.pallas_call(
        _conv_sample_kernel,
        grid=(b // bs,),
        in_specs=[pl.BlockSpec((bs, t, D_A), lambda bi: (bi, 0, C_A // D_A)),
                  pl.BlockSpec((bs, t, D_A), lambda bi: (bi, 0, C_AG // D_A)),
                  pl.BlockSpec((bs, hist, D_A), lambda bi: (bi, 0, 0)),
                  pl.BlockSpec((CONV_W, D_A), lambda bi: (0, 0)),
                  pl.BlockSpec((1, D_A), lambda bi: (0, 0)),
                  pl.BlockSpec((1, D_A), lambda bi: (0, 0)),
                  pl.BlockSpec((1, D_A), lambda bi: (0, 0))],
        out_specs=[pl.BlockSpec((bs, t, D_A), lambda bi: (bi, 0, 0)),
                   pl.BlockSpec((bs, hist, D_A), lambda bi: (bi, 0, 0))],
        out_shape=[jax.ShapeDtypeStruct((b, t, D_A), F32),
                   jax.ShapeDtypeStruct((b, hist, D_A), F32)],
        scratch_shapes=[pltpu.VMEM((bs, hist + t + 2, D_A), F32)],
        compiler_params=_cp(1),
        name="conv_sample",
    )(z, z, state, ck, cb, lg, lb)


def _sgu_prompt_kernel(u_ref, v_ref, lg_ref, lb_ref, ws_ref, bs_ref, y_ref):
    tt = u_ref.shape[1]
    vv = _ln(v_ref[0], lg_ref[...], lb_ref[...])
    row = lax.broadcasted_iota(I32, (CHUNK, CHUNK), 0)
    col = lax.broadcasted_iota(I32, (CHUNK, CHUNK), 1)
    lane_grp = lax.broadcasted_iota(I32, (CHUNK, D_C), 1) // (D_C // N_CGROUPS)
    ws = [jnp.where(col <= row, ws_ref[g], 0.0).astype(BF16) for g in range(N_CGROUPS)]
    for c in range(tt // CHUNK):
        vc = vv[c * CHUNK:(c + 1) * CHUNK].astype(BF16)
        mixed = bs_ref[...]
        for g in range(N_CGROUPS):
            mixed = mixed + _dot(ws[g], jnp.where(lane_grp == g, vc, jnp.zeros_like(vc)))
        y_ref[0, c * CHUNK:(c + 1) * CHUNK, :] = (u_ref[0, c * CHUNK:(c + 1) * CHUNK, :] * mixed).astype(y_ref.dtype)


def _sgu_prompt(z, lg, lb, w_s, bs_tab, tt=512):
    b, t, _ = z.shape
    return pl.pallas_call(
        _sgu_prompt_kernel,
        grid=(b, t // tt),
        in_specs=[pl.BlockSpec((1, tt, D_C), lambda bi, ti: (bi, ti, C_U // D_C)),
                  pl.BlockSpec((1, tt, D_C), lambda bi, ti: (bi, ti, C_V // D_C)),
                  pl.BlockSpec((1, D_C), lambda bi, ti: (0, 0)),
                  pl.BlockSpec((1, D_C), lambda bi, ti: (0, 0)),
                  pl.BlockSpec((N_CGROUPS, CHUNK, CHUNK), lambda bi, ti: (0, 0, 0)),
                  pl.BlockSpec((CHUNK, D_C), lambda bi, ti: (0, 0))],
        out_specs=pl.BlockSpec((1, tt, D_C), lambda bi, ti: (bi, ti, 0)),
        out_shape=jax.ShapeDtypeStruct((b, t, D_C), BF16),
        compiler_params=_cp(2),
        name="sgu_prompt",
    )(z, z, lg, lb, w_s, bs_tab)


def _sgu_sample_kernel(u_ref, v_ref, lg_ref, lb_ref, wl_ref, bs_ref, y_ref, vc_ref):
    bs, t, _ = u_ref.shape
    vv = _ln(v_ref[...], lg_ref[...], lb_ref[...])
    vc_ref[...] = vv
    tpos = lax.broadcasted_iota(I32, (t, D_C), 0)
    mixed = jnp.zeros((bs, t, D_C), F32) + bs_ref[...]
    for s in range(t):
        w = jnp.where(tpos >= s, wl_ref[s], 0.0)
        mixed = mixed + w * vv[:, s:s + 1, :]
    y_ref[...] = u_ref[...] * mixed


def _sgu_sample(z, lg, lb, wl_tab, bs_tab):
    b, t, _ = z.shape
    return pl.pallas_call(
        _sgu_sample_kernel,
        grid=(1,),
        in_specs=[pl.BlockSpec((b, t, D_C), lambda i: (0, 0, C_U // D_C)),
                  pl.BlockSpec((b, t, D_C), lambda i: (0, 0, C_V // D_C)),
                  pl.BlockSpec((1, D_C), lambda i: (0, 0)),
                  pl.BlockSpec((1, D_C), lambda i: (0, 0)),
                  pl.BlockSpec((t, t, D_C), lambda i: (0, 0, 0)),
                  pl.BlockSpec((t, D_C), lambda i: (0, 0))],
        out_specs=[pl.BlockSpec((b, t, D_C), lambda i: (0, 0, 0)),
                   pl.BlockSpec((b, t, D_C), lambda i: (0, 0, 0))],
        out_shape=[jax.ShapeDtypeStruct((b, t, D_C), F32),
                   jax.ShapeDtypeStruct((b, t, D_C), F32)],
        compiler_params=_cp(1),
        name="sgu_sample",
    )(z, z, lg, lb, wl_tab, bs_tab)


def _bias_kernel(tab_ref, o_ref):
    rows, cols = KEY_BLK, 2 * KEY_BLK
    i = lax.broadcasted_iota(I32, (rows, cols), 0)
    j = lax.broadcasted_iota(I32, (rows, cols), 1)
    n = jnp.maximum(KEY_BLK + i - j, 0)
    max_exact = N_BUCKETS // 2
    nf = jnp.maximum(n, 1).astype(F32)
    large = max_exact + (jnp.log(nf / max_exact) / math.log(MAX_DIST / max_exact) * (N_BUCKETS - max_exact)).astype(I32)
    large = jnp.minimum(large, N_BUCKETS - 1)
    bucket = jnp.where(n < max_exact, n, large)
    for h in range(N_HEADS):
        acc = jnp.zeros((rows, cols), F32)
        for bkt in range(N_BUCKETS):
            acc = jnp.where(bucket == bkt, tab_ref[bkt, h], acc)
        o_ref[h // GROUP, (h % GROUP) * rows:(h % GROUP + 1) * rows, :] = (acc - tab_ref[N_BUCKETS - 1, h]) * LOG2E


def _bias_near(rel_table):
    return pl.pallas_call(
        _bias_kernel,
        grid=(1,),
        in_specs=[pl.BlockSpec(memory_space=pltpu.SMEM)],
        out_specs=pl.BlockSpec((N_KV, GROUP * KEY_BLK, 2 * KEY_BLK), lambda i: (0, 0, 0)),
        out_shape=jax.ShapeDtypeStruct((N_KV, GROUP * KEY_BLK, 2 * KEY_BLK), F32),
        compiler_params=_cp(1),
        name="bias_near",
    )(rel_table)


def _score_keys(s):
    s = jnp.where(s == 0.0, 0.0, s)
    bits = lax.bitcast_convert_type(s, I32)
    return bits ^ (jnp.right_shift(bits, 31) & 0x7FFFFFFF)


def _lane_top2(s, m1, m2):
    for blk in [s[:, c * LANES:(c + 1) * LANES] for c in range(s.shape[1] // LANES)]:
        m2 = jnp.maximum(m2, jnp.minimum(m1, blk))
        m1 = jnp.maximum(m1, blk)
    return m1, m2


def _topk_threshold(load_blk, n_iter, unroll, rows, k, idx_bits, cut_scr, bounds=None, parts=1):
    lane = lax.broadcasted_iota(I32, (rows, KEY_BLK), 1)
    groups = range(parts)
    if bounds is None:
        first_bit = 0
        thr0 = [jnp.full((rows, 1), INT_MIN, I32) for _ in groups]
    else:
        shared = [jnp.min(lax.clz(lo ^ hi)) for lo, hi in bounds]
        first_bit = jnp.minimum(functools.reduce(jnp.minimum, shared), 31)
        keep = ~(jnp.left_shift(jnp.int32(2), 31 - first_bit) - 1)
        thr0 = [((hi ^ INT_MIN) & keep) ^ INT_MIN for _, hi in bounds]

    def count(preds):
        accs = []
        for p in groups:
            def body(jj, acc, p=p):
                for u in range(unroll):
                    j = jj * unroll + u
                    acc = jnp.where(preds[p](load_blk(p, j), j), acc + 1.0, acc)
                return acc
            accs.append(lax.fori_loop(0, n_iter, body, jnp.zeros((rows, KEY_BLK), F32)))
        return [jnp.sum(a, axis=1, keepdims=True) for a in accs]

    def ge(xs):
        return [lambda kb, j, x=x: kb >= x for x in xs]

    kf = float(k)

    def bit_body(i, thr):
        cand = [t ^ jnp.left_shift(jnp.int32(1), 31 - i) for t in thr]
        cnt = count(ge(cand))
        return [jnp.where(cnt[p] >= kf, cand[p], thr[p]) for p in groups]

    thr = lax.fori_loop(first_bit, 32, bit_body, thr0)
    cnt_gt = count([lambda kb, j, t=t: kb > t for t in thr])
    need = [kf - c for c in cnt_gt]
    if cut_scr is None:
        return thr, need
    assert parts == 1
    thr, need = thr[0], need[0]
    cnt_ge = count(ge([thr]))[0]
    cut_scr[...] = jnp.full((rows, 1), 2 ** idx_bits, I32)

    @pl.when(jnp.max(cnt_ge) > kf)
    def _():
        def idx_body(i, r):
            cand = r + jnp.left_shift(jnp.int32(1), idx_bits - 1 - i)
            cnt = count([lambda kb, j: (kb == thr) & (j * KEY_BLK + lane < cand)])[0]
            return jnp.where(cnt < need, cand, r)
        r = lax.fori_loop(0, idx_bits, idx_body, jnp.zeros((rows, 1), I32))
        cut_scr[...] = r + 1

    return thr, cut_scr[...]


def _select_mask(keys, thr, cut, gidx):
    return (keys > thr) | ((keys == thr) & (gidx < cut))


def _stack_heads(q, n, rows):
    half = lax.broadcasted_iota(I32, (rows, LANES), 1) // HEAD_DIM
    tiles = []
    for g in range(GROUP):
        h = n * GROUP + g
        col = q[:, (h // 2) * LANES:(h // 2 + 1) * LANES]
        if h % 2 != n:
            col = pltpu.roll(col, HEAD_DIM, 1)
        tiles.append(jnp.where(half == n, col, 0.0))
    return jnp.concatenate(tiles, axis=0)


def _unstack_heads(o_groups, rows):
    half = lax.broadcasted_iota(I32, (rows, LANES), 1) // HEAD_DIM
    cols = []
    for c in range(N_HEADS // 2):
        parts = []
        for h in (2 * c, 2 * c + 1):
            n, g = h // GROUP, h % GROUP
            tile = o_groups[n][g * rows:(g + 1) * rows, :]
            if h % 2 != n:
                tile = pltpu.roll(tile, HEAD_DIM, 1)
            parts.append(tile)
        cols.append(jnp.where(half == 0, parts[0], parts[1]))
    return jnp.concatenate(cols, axis=1)


def _dsa_prompt_kernel(q_ref, qi_ref, wi_ref, ki_ref, k_ref, v_ref, bias_ref, o_ref,
                       key_scr, qs_scr, mpart_scr, mfull_scr, lpart_scr, acc_scr, p_scr,
                       lgfar_scr, lgnear_scr, *, topk):
    pair = pl.program_id(1)
    rows = KEY_BLK
    chunk = FAR_BLKS * KEY_BLK
    row_i = lax.broadcasted_iota(I32, (rows, 1), 0)
    lane_c = lax.broadcasted_iota(I32, (rows, chunk), 1)
    n_chunks = (pair * Q_PAIR) // FAR_BLKS + 1
    assert FAR_BLKS % Q_PAIR == 0

    def scores(s):
        sub = slice(s * rows, (s + 1) * rows)
        tpos = (pair * Q_PAIR + s) * rows + row_i
        qi = qi_ref[0, sub, :]
        lane_head = lax.broadcasted_iota(I32, qi.shape, 1) // D_IDX
        qim = [jnp.where(lane_head == h, qi, 0.0).astype(BF16) for h in range(N_IDX_HEADS)]
        wi = wi_ref[0, sub, :]

        def score_body(c, top2):
            kc = ki_ref[0, pl.ds(pl.multiple_of(c * chunk, chunk), chunk), :].astype(BF16)
            sc = jnp.zeros((rows, chunk), F32)
            for h in range(N_IDX_HEADS):
                sc = sc + wi[:, h:h + 1] * jnp.maximum(_dot_nt(qim[h], kc), 0.0)
            sc = jnp.where(c * chunk + lane_c <= tpos, sc, -jnp.inf)
            keys = _score_keys(sc)
            for j in range(FAR_BLKS):
                key_scr[s, c * FAR_BLKS + j] = keys[:, j * KEY_BLK:(j + 1) * KEY_BLK]
            return _lane_top2(sc, *top2)

        ninf = jnp.full((rows, LANES), -jnp.inf, F32)
        m1, m2 = lax.fori_loop(0, n_chunks, score_body, (ninf, ninf))
        return (_score_keys(jnp.min(m2, axis=1, keepdims=True)), _score_keys(jnp.max(m1, axis=1, keepdims=True)))

    assert topk <= 2 * LANES
    bounds = [scores(s) for s in range(Q_PAIR)]
    thr_all, need_all = _topk_threshold(lambda s, j: key_scr[s, j], n_chunks, FAR_BLKS, rows, topk, 0, None,
                                        bounds, parts=Q_PAIR)

    for s in range(Q_PAIR):
        sub = slice(s * rows, (s + 1) * rows)
        _dsa_prompt_attend(pair * Q_PAIR + s, thr_all[s], need_all[s], q_ref.at[0, sub, :], k_ref, v_ref,
                           bias_ref, o_ref.at[0, sub, :], key_scr.at[s], qs_scr, mpart_scr, mfull_scr, lpart_scr,
                           acc_scr, p_scr, lgfar_scr, lgnear_scr)


def _dsa_prompt_attend(qb, thr, need, q_ref, k_ref, v_ref, bias_ref, o_ref, key_scr, qs_scr, mpart_scr, mfull_scr,
                       lpart_scr, acc_scr, p_scr, lgfar_scr, lgnear_scr):
    rows = KEY_BLK
    chunk = FAR_BLKS * KEY_BLK
    tpos = qb * rows + lax.broadcasted_iota(I32, (rows, 1), 0)

    ri = lax.broadcasted_iota(I32, (KEY_BLK, KEY_BLK), 0)
    ci = lax.broadcasted_iota(I32, (KEY_BLK, KEY_BLK), 1)
    prefix_mat = jnp.where(ri <= ci, 1.0, 0.0).astype(BF16)
    ones_mat = jnp.ones((KEY_BLK, KEY_BLK), BF16)

    def select_block(kb, run):
        eq = kb == thr
        e = jnp.where(eq, 1.0, 0.0).astype(BF16)
        sel = (kb > thr) | (eq & (run + _dot(e, prefix_mat) <= need))
        return sel, run + _dot(e, ones_mat)

    q = q_ref[...] * (HEAD_DIM ** -0.5 * LOG2E)
    for n in range(N_KV):
        qs_scr[n] = _stack_heads(q, n, rows).astype(BF16)
    mpart_scr[...] = jnp.full(mpart_scr.shape, NEG_BIG, F32)
    lpart_scr[...] = jnp.zeros(lpart_scr.shape, F32)
    acc_scr[...] = jnp.zeros(acc_scr.shape, F32)

    def lane_blocks(t):
        return [t[:, c * LANES:(c + 1) * LANES] for c in range(t.shape[1] // LANES)]

    def fold(vals, op):
        while len(vals) > 1:
            vals = [op(vals[i], vals[i + 1]) for i in range(0, len(vals), 2)]
        return vals[0]

    def sweep_max(kc, neg, with_bias, lg_ref):
        for n in range(N_KV):
            lg = _dot_nt(qs_scr[n], kc)
            for g in range(GROUP):
                sl = slice(g * rows, (g + 1) * rows)
                t = lg[sl] + neg
                if with_bias:
                    t = t + bias_ref[n, sl, :]
                lg_ref[n, sl, :] = t
                mpart_scr[n, sl, :] = jnp.maximum(mpart_scr[n, sl, :], fold(lane_blocks(t), jnp.maximum))

    def sweep_acc(vc, lg_ref):
        width = vc.shape[0]
        for n in range(N_KV):
            for g in range(GROUP):
                sl = slice(g * rows, (g + 1) * rows)
                mf = mfull_scr[n, sl, :]
                ps = [jnp.exp2(tb - mf) for tb in lane_blocks(lg_ref[n, sl, :])]
                lpart_scr[n, sl, :] = lpart_scr[n, sl, :] + fold(ps, jnp.add)
                p_scr[sl, 0:width] = jnp.concatenate(ps, axis=1).astype(BF16)
            acc_scr[n] = acc_scr[n] + _dot(p_scr[:, 0:width], vc)

    n_far = (qb + 2) // FAR_BLKS

    def far_max(c, run):
        start = pl.multiple_of(c * chunk, chunk)
        kc = k_ref[0, pl.ds(start, chunk), :].astype(BF16)
        negs = []
        for j in range(FAR_BLKS):
            blk = c * FAR_BLKS + j
            is_far = blk < qb - 1
            sel, run_next = select_block(key_scr[blk], run)
            run = jnp.where(is_far, run_next, run)
            negs.append(jnp.where(sel & is_far, 0.0, NEG_BIG))
        sweep_max(kc, jnp.concatenate(negs, axis=1), False, lgfar_scr.at[c])
        return run

    run = lax.fori_loop(0, n_far, far_max, jnp.zeros((rows, KEY_BLK), F32))

    pb = jnp.maximum(qb - 1, 0)
    p0 = pl.multiple_of(pb * rows, rows)
    q0 = pl.multiple_of(qb * rows, rows)
    lane_b = lax.broadcasted_iota(I32, (rows, KEY_BLK), 1)
    sel_prev, run_next = select_block(key_scr[pb], run)
    run = jnp.where(qb > 0, run_next, run)
    sel_prev = sel_prev & (qb > 0)
    sel_diag, _ = select_block(key_scr[qb], run)
    sel_diag = sel_diag & (qb * rows + lane_b <= tpos)
    neg_near = jnp.concatenate([jnp.where(sel_prev, 0.0, NEG_BIG), jnp.where(sel_diag, 0.0, NEG_BIG)], axis=1)

    def near_kv(ref):
        return jnp.concatenate([ref[0, pl.ds(p0, rows), :], ref[0, pl.ds(q0, rows), :]], axis=0).astype(BF16)

    sweep_max(near_kv(k_ref), neg_near, True, lgnear_scr)
    for n in range(N_KV):
        mfull_scr[n] = jnp.broadcast_to(jnp.max(mpart_scr[n], axis=1, keepdims=True), mfull_scr.shape[1:])

    def far_acc(c, carry):
        start = pl.multiple_of(c * chunk, chunk)
        sweep_acc(v_ref[0, pl.ds(start, chunk), :].astype(BF16), lgfar_scr.at[c])
        return carry

    lax.fori_loop(0, n_far, far_acc, 0)
    sweep_acc(near_kv(v_ref), lgnear_scr)

    outs = [acc_scr[n] / jnp.sum(lpart_scr[n], axis=1, keepdims=True) for n in range(N_KV)]
    o_ref[...] = _unstack_heads(outs, rows).astype(o_ref.dtype)


def _dsa_prompt(z, bias_near):
    b, t, _ = z.shape
    nb = t // KEY_BLK
    assert nb % FAR_BLKS == 0
    topk = min(TOPK_MAX, t // 4)
    grows = GROUP * KEY_BLK
    chunk = FAR_BLKS * KEY_BLK
    qrows = Q_PAIR * KEY_BLK
    return pl.pallas_call(
        functools.partial(_dsa_prompt_kernel, topk=topk),
        grid=(b, nb // Q_PAIR),
        in_specs=[pl.BlockSpec((1, qrows, D_ATT), lambda bi, qi: (bi, qi, C_Q // D_ATT)),
                  pl.BlockSpec((1, qrows, 256), lambda bi, qi: (bi, qi, C_QI // 256)),
                  pl.BlockSpec((1, qrows, LANES), lambda bi, qi: (bi, qi, C_WI // LANES)),
                  pl.BlockSpec((1, t, 256), lambda bi, qi: (bi, 0, C_KI4 // 256)),
                  pl.BlockSpec((1, t, LANES), lambda bi, qi: (bi, 0, C_K // LANES)),
                  pl.BlockSpec((1, t, LANES), lambda bi, qi: (bi, 0, C_VV // LANES)),
                  pl.BlockSpec((N_KV, grows, 2 * KEY_BLK), lambda bi, qi: (0, 0, 0))],
        out_specs=pl.BlockSpec((1, qrows, D_ATT), lambda bi, qi: (bi, qi, 0)),
        out_shape=jax.ShapeDtypeStruct((b, t, D_ATT), BF16),
        scratch_shapes=[pltpu.VMEM((Q_PAIR, nb, KEY_BLK, KEY_BLK), I32),
                        pltpu.VMEM((N_KV, grows, LANES), BF16),
                        pltpu.VMEM((N_KV, grows, LANES), F32),
                        pltpu.VMEM((N_KV, grows, LANES), F32),
                        pltpu.VMEM((N_KV, grows, LANES), F32),
                        pltpu.VMEM((N_KV, grows, LANES), F32),
                        pltpu.VMEM((grows, chunk), BF16),
                        pltpu.VMEM((nb // FAR_BLKS, N_KV, grows, chunk), F32),
                        pltpu.VMEM((N_KV, grows, 2 * KEY_BLK), F32)],
        compiler_params=_cp(2),
        name="dsa_prompt",
    )(z, z, z, z, z, z, bias_near)


def _dsa_sample_score_kernel(pt_ref, qi_ref, wi_ref, kin_ref, *rest):
    n_pages = len(rest) - 1
    pages, o_ref = rest[:n_pages], rest[n_pages]
    t = qi_ref.shape[1]
    qi = qi_ref[0]
    wi = wi_ref[0]
    qs = jnp.concatenate([qi[:, h * D_IDX:(h + 1) * D_IDX] for h in range(N_IDX_HEADS)], axis=0).astype(BF16)
    k_past_t = jnp.concatenate([pg[0, 0] for pg in pages], axis=1).astype(BF16)
    k_new = jnp.concatenate([kin_ref[0][:, :D_IDX], jnp.zeros((KEY_BLK - t, D_IDX), F32)], axis=0).astype(BF16)

    def score(d):
        s = jnp.zeros((t, d.shape[1]), F32)
        for h in range(N_IDX_HEADS):
            s = s + wi[:, h:h + 1] * jnp.maximum(d[h * t:(h + 1) * t], 0.0)
        return s

    keys_past = _score_keys(score(_dot(qs, k_past_t)))
    for j in range(n_pages):
        o_ref[j] = keys_past[:, j * KEY_BLK:(j + 1) * KEY_BLK]
    s_new = score(_dot_nt(qs, k_new))
    lane = lax.broadcasted_iota(I32, s_new.shape, 1)
    trow = lax.broadcasted_iota(I32, s_new.shape, 0)
    o_ref[n_pages] = _score_keys(jnp.where(lane <= trow, s_new, -jnp.inf))


def _page_specs(n_pages, layer, feat):
    return [pl.BlockSpec((1, 1, feat, PAGE_SIZE), lambda bi, pt, j=j: (layer, pt[bi, j], 0, 0))
            for j in range(n_pages)]


def _dsa_sample_scores(z, cache_kidx, page_table, layer):
    b, t, _ = z.shape
    n_pages = page_table.shape[1]
    nblk = n_pages + 1
    grid_spec = pltpu.PrefetchScalarGridSpec(
        num_scalar_prefetch=1,
        grid=(b,),
        in_specs=[pl.BlockSpec((1, t, 256), lambda bi, pt: (bi, 0, C_QI // 256)),
                  pl.BlockSpec((1, t, LANES), lambda bi, pt: (bi, 0, C_WI // LANES)),
                  pl.BlockSpec((1, t, 256), lambda bi, pt: (bi, 0, C_KI4 // 256))]
        + _page_specs(n_pages, layer, D_IDX),
        out_specs=pl.BlockSpec((nblk, t, KEY_BLK), lambda bi, pt: (0, bi, 0)),
    )
    return pl.pallas_call(
        _dsa_sample_score_kernel,
        grid_spec=grid_spec,
        out_shape=jax.ShapeDtypeStruct((nblk, b * t, KEY_BLK), I32),
        compiler_params=_cp(1),
        name="dsa_sample_scores",
    )(page_table, z, z, z, *([cache_kidx] * n_pages))


def _dsa_sample_topk_kernel(key_ref, thr_ref, cut_ref, cut_scr, *, topk):
    nblk, rows, _ = key_ref.shape
    thr, cut = _topk_threshold(lambda p, j: key_ref[j], 1, nblk, rows, topk, 13, cut_scr)
    thr_ref[...] = thr
    cut_ref[...] = cut


def _dsa_sample_topk(keys, topk, rows=128):
    nblk, m, _ = keys.shape
    return pl.pallas_call(
        functools.partial(_dsa_sample_topk_kernel, topk=topk),
        grid=(m // rows,),
        in_specs=[pl.BlockSpec((nblk, rows, KEY_BLK), lambda i: (0, i, 0))],
        out_specs=[pl.BlockSpec((rows, 1), lambda i: (i, 0)),
                   pl.BlockSpec((rows, 1), lambda i: (i, 0))],
        out_shape=[jax.ShapeDtypeStruct((m, 1), I32), jax.ShapeDtypeStruct((m, 1), I32)],
        scratch_shapes=[pltpu.VMEM((rows, 1), I32)],
        compiler_params=_cp(1),
        name="dsa_sample_topk",
    )(keys)


def _dsa_sample_attn_kernel(pt_ref, q_ref, kn_ref, vn_ref, key_ref, thr_ref, cut_ref, bias_ref, *rest):
    n_pages = (len(rest) - 1) // 2
    kpages, vpages, o_ref = rest[:n_pages], rest[n_pages:2 * n_pages], rest[2 * n_pages]
    t = q_ref.shape[1]
    past = n_pages * PAGE_SIZE
    thr = thr_ref[...]
    cut = cut_ref[...]
    q = q_ref[0] * (HEAD_DIM ** -0.5 * LOG2E)
    trow = lax.broadcasted_iota(I32, (t, 1), 0)

    n_far = n_pages - 1
    kt_far = jnp.concatenate([pg[0, 0] for pg in kpages[:n_far]], axis=1).astype(BF16)
    vt_far = jnp.concatenate([pg[0, 0] for pg in vpages[:n_far]], axis=1).astype(BF16)
    keys_far = jnp.concatenate([key_ref[j] for j in range(n_far)], axis=1)
    g_far = lax.broadcasted_iota(I32, keys_far.shape, 1)
    neg_far = jnp.where(_select_mask(keys_far, thr, cut, g_far), 0.0, NEG_BIG)

    pad = jnp.zeros((KEY_BLK - t, LANES), F32)
    kt_last = kpages[n_far][0, 0].astype(BF16)
    vt_last = vpages[n_far][0, 0].astype(BF16)
    k_new = jnp.concatenate([kn_ref[0], pad], axis=0).astype(BF16)
    v_new = jnp.concatenate([vn_ref[0], pad], axis=0).astype(BF16)
    lane_b = lax.broadcasted_iota(I32, (t, KEY_BLK), 1)
    sel_last = _select_mask(key_ref[n_far], thr, cut, n_far * KEY_BLK + lane_b)
    sel_new = _select_mask(key_ref[n_pages], thr, cut, past + lane_b) & (lane_b <= trow)
    neg_near = jnp.concatenate([jnp.where(sel_last, 0.0, NEG_BIG), jnp.where(sel_new, 0.0, NEG_BIG)], axis=1)

    qs = jnp.concatenate([_stack_heads(q, n, t) for n in range(N_KV)], axis=0).astype(BF16)
    bias = jnp.concatenate([bias_ref[n] for n in range(N_KV)], axis=0)
    lg_far = _dot(qs, kt_far) + jnp.concatenate([neg_far] * N_HEADS, axis=0)
    lg_near = (jnp.concatenate([_dot(qs, kt_last), _dot_nt(qs, k_new)], axis=1) + bias
               + jnp.concatenate([neg_near] * N_HEADS, axis=0))
    m = jnp.maximum(jnp.max(lg_far, axis=1, keepdims=True), jnp.max(lg_near, axis=1, keepdims=True))
    p_far = jnp.exp2(lg_far - m)
    p_near = jnp.exp2(lg_near - m)
    l = jnp.sum(p_far, axis=1, keepdims=True) + jnp.sum(p_near, axis=1, keepdims=True)
    p_near = p_near.astype(BF16)
    o = (_dot_nt(p_far.astype(BF16), vt_far) + _dot_nt(p_near[:, :KEY_BLK], vt_last)
         + _dot(p_near[:, KEY_BLK:], v_new)) / l
    o_ref[0] = _unstack_heads([o[n * GROUP * t:(n + 1) * GROUP * t] for n in range(N_KV)], t)


def _dsa_sample_attn(z, keys, thr, cut, cache_k, cache_v, page_table, layer, bias_s):
    b, t, _ = z.shape
    n_pages = page_table.shape[1]
    nblk = n_pages + 1
    grid_spec = pltpu.PrefetchScalarGridSpec(
        num_scalar_prefetch=1,
        grid=(b,),
        in_specs=[pl.BlockSpec((1, t, D_ATT), lambda bi, pt: (bi, 0, C_Q // D_ATT)),
                  pl.BlockSpec((1, t, LANES), lambda bi, pt: (bi, 0, C_K // LANES)),
                  pl.BlockSpec((1, t, LANES), lambda bi, pt: (bi, 0, C_VV // LANES)),
                  pl.BlockSpec((nblk, t, KEY_BLK), lambda bi, pt: (0, bi, 0)),
                  pl.BlockSpec((t, 1), lambda bi, pt: (bi, 0)),
                  pl.BlockSpec((t, 1), lambda bi, pt: (bi, 0)),
                  pl.BlockSpec((N_KV, GROUP * t, 2 * KEY_BLK), lambda bi, pt: (0, 0, 0))]
        + _page_specs(n_pages, layer, LANES) + _page_specs(n_pages, layer, LANES),
        out_specs=pl.BlockSpec((1, t, D_ATT), lambda bi, pt: (bi, 0, 0)),
    )
    return pl.pallas_call(
        _dsa_sample_attn_kernel,
        grid_spec=grid_spec,
        out_shape=jax.ShapeDtypeStruct((b, t, D_ATT), F32),
        compiler_params=_cp(1),
        name="dsa_sample_attn",
    )(page_table, z, z, z, keys, thr, cut, bias_s, *([cache_k] * n_pages), *([cache_v] * n_pages))


def _merge_kernel(ya_ref, yb_ref, yc_ref, ga_ref, gb_ref, gc_ref, x_ref, g1_ref, sh2_ref, sc2_ref,
                  wpa_ref, wpb_ref, wpc_ref, wo_ref, lng_ref, lnb_ref, wrh_ref, wrl_ref, rb_ref,
                  x1_ref, h2_ref, gate_ref, *, alpha):
    gbk, rbk, d = x_ref.shape
    m = gbk * rbk

    def flat(ref):
        return ref[...].reshape(m, ref.shape[-1])

    merged = (_sigmoid(flat(ga_ref)) * _dot(flat(ya_ref).astype(BF16), wpa_ref[...])
              + _sigmoid(flat(gb_ref)) * _dot(flat(yb_ref).astype(BF16), wpb_ref[...])
              + _sigmoid(flat(gc_ref)) * _dot(flat(yc_ref).astype(BF16), wpc_ref[...]))
    mix = _dot(merged.astype(BF16), wo_ref[...]).reshape(gbk, rbk, d)
    x1 = _ln(alpha * x_ref[...] + (1.0 + g1_ref[...]) * mix, lng_ref[...], lnb_ref[...])
    x1_ref[...] = x1
    h2 = x1 * (1.0 + sc2_ref[...]) + sh2_ref[...]
    h2_ref[...] = h2.astype(h2_ref.dtype)

    h2f = h2.reshape(m, d)
    hi = h2f.astype(BF16)
    lo = (h2f - hi.astype(F32)).astype(BF16)
    logits = _dot_nt(wrh_ref[...], hi) + _dot_nt(wrl_ref[...], hi) + _dot_nt(wrh_ref[...], lo)
    scores = _sigmoid(logits)
    sel = scores + rb_ref[...]
    row = lax.broadcasted_iota(I32, (N_EXPERTS, m), 0).astype(F32)
    best = jnp.zeros((1, m), F32)
    best_score = None
    for g in range(N_GROUPS):
        r = [sel[g * EXPERTS_PER_GROUP + i:g * EXPERTS_PER_GROUP + i + 1] for i in range(EXPERTS_PER_GROUP)]
        top2 = None
        for i in range(EXPERTS_PER_GROUP):
            for j in range(i + 1, EXPERTS_PER_GROUP):
                pair = r[i] + r[j]
                top2 = pair if top2 is None else jnp.maximum(top2, pair)
        if best_score is None:
            best_score = top2
        else:
            better = top2 > best_score
            best = jnp.where(better, float(g), best)
            best_score = jnp.where(better, top2, best_score)
    lo_row = best * EXPERTS_PER_GROUP
    in_group = (row >= lo_row) & (row < lo_row + EXPERTS_PER_GROUP)
    masked = jnp.where(in_group, sel, -jnp.inf)
    m1 = jnp.max(masked, axis=0, keepdims=True)
    i1 = jnp.min(jnp.where(masked == m1, row, float(N_EXPERTS)), axis=0, keepdims=True)
    masked2 = jnp.where(row == i1, -jnp.inf, masked)
    m2 = jnp.max(masked2, axis=0, keepdims=True)
    i2 = jnp.min(jnp.where(masked2 == m2, row, float(N_EXPERTS)), axis=0, keepdims=True)
    w1 = jnp.sum(jnp.where(row == i1, scores, 0.0), axis=0, keepdims=True)
    w2 = jnp.sum(jnp.where(row == i2, scores, 0.0), axis=0, keepdims=True)
    tot = w1 + w2
    gate_ref[...] = jnp.where(row == i1, w1 / tot, 0.0) + jnp.where(row == i2, w2 / tot, 0.0)


def _merge(ya, yb, yc, z, x3, mod, w_pa, w_pb, w_pc, w_o, lng, lnb, wr_hi, wr_lo, rbias, gb, rb, act_dtype, alpha):
    g, r, d = x3.shape
    rt = r // rb

    def act(width, col):
        return pl.BlockSpec((gb, rb, width), lambda gi, ri: (gi, ri, col))

    def modspec(col):
        return pl.BlockSpec((gb, 1, d), lambda gi, ri: (gi, 0, col))

    def full(a):
        return pl.BlockSpec(a.shape, lambda gi, ri: (0,) * a.ndim)

    return pl.pallas_call(
        functools.partial(_merge_kernel, alpha=alpha),
        grid=(g // gb, rt),
        in_specs=[act(D_A, 0), act(D_ATT, 0), act(D_C, 0),
                  act(d, C_GA // d), act(d, C_GB // d), act(d, C_GC // d),
                  act(d, 0), modspec(2), modspec(3), modspec(4),
                  full(w_pa), full(w_pb), full(w_pc), full(w_o), full(lng), full(lnb),
                  full(wr_hi), full(wr_lo), full(rbias)],
        out_specs=[act(d, 0), act(d, 0),
                   pl.BlockSpec((N_EXPERTS, gb * rb), lambda gi, ri: (0, gi * rt + ri))],
        out_shape=[jax.ShapeDtypeStruct((g, r, d), F32),
                   jax.ShapeDtypeStruct((g, r, d), act_dtype),
                   jax.ShapeDtypeStruct((N_EXPERTS, g * r), F32)],
        compiler_params=_cp(2),
        name="merge",
    )(ya, yb, yc, z, z, z, x3, mod, mod, mod, w_pa, w_pb, w_pc, w_o, lng, lnb, wr_hi, wr_lo, rbias)


def _moe_kernel(h_ref, gate_ref, x1_ref, g2_ref, wg_ref, wu_ref, wd_ref, lng_ref, lnb_ref, o_ref, acc_scr, *, alpha):
    e = pl.program_id(2)
    gbk, rbk, d = h_ref.shape
    m = gbk * rbk

    @pl.when(e == 0)
    def _():
        acc_scr[...] = jnp.zeros(acc_scr.shape, F32)

    h = h_ref[...].reshape(m, d).astype(BF16)
    a = _dot(h, wg_ref[0, 0])
    u = _dot(h, wu_ref[0, 0])
    he = (a * _sigmoid(a)) * u
    gate = gate_ref[...]
    lane = lax.broadcasted_iota(I32, gate.shape, 1)
    gcol = jnp.sum(jnp.where(lane == e, gate, 0.0), axis=1, keepdims=True)
    acc_scr[...] += gcol * _dot(he.astype(BF16), wd_ref[0, 0])

    @pl.when(e == pl.num_programs(2) - 1)
    def _():
        y = acc_scr[...].reshape(gbk, rbk, d)
        o_ref[...] = _ln(alpha * x1_ref[...] + (1.0 + g2_ref[...]) * y, lng_ref[...], lnb_ref[...])


def _moe(h2, gate, x1, mod, w_g, w_u, w_d, lng, lnb, layer, gb, rb, alpha):
    g, r, d = x1.shape
    rt = r // rb
    n_e = w_g.shape[1]
    return pl.pallas_call(
        functools.partial(_moe_kernel, alpha=alpha),
        grid=(g // gb, rt, n_e),
        in_specs=[pl.BlockSpec((gb, rb, d), lambda gi, ri, e: (gi, ri, 0)),
                  pl.BlockSpec((gb * rb, n_e), lambda gi, ri, e: (gi * rt + ri, 0)),
                  pl.BlockSpec((gb, rb, d), lambda gi, ri, e: (gi, ri, 0)),
                  pl.BlockSpec((gb, 1, d), lambda gi, ri, e: (gi, 0, 5)),
                  pl.BlockSpec((1, 1, d, D_EXPERT), lambda gi, ri, e: (layer, e, 0, 0)),
                  pl.BlockSpec((1, 1, d, D_EXPERT), lambda gi, ri, e: (layer, e, 0, 0)),
                  pl.BlockSpec((1, 1, D_EXPERT, d), lambda gi, ri, e: (layer, e, 0, 0)),
                  pl.BlockSpec((1, d), lambda gi, ri, e: (0, 0)),
                  pl.BlockSpec((1, d), lambda gi, ri, e: (0, 0))],
        out_specs=pl.BlockSpec((gb, rb, d), lambda gi, ri, e: (gi, ri, 0)),
        out_shape=jax.ShapeDtypeStruct((g, r, d), F32),
        scratch_shapes=[pltpu.VMEM((gb * rb, d), F32)],
        compiler_params=_cp(3),
        name="moe",
    )(h2, gate, x1, mod, w_g, w_u, w_d, lng, lnb)


def _pad_cols(n, dtype, like):
    return jnp.zeros(like.shape[:-1] + (n,), dtype)


def _relayout_in(w):
    def sl(o, n):
        return w[..., o:o + n]
    ki = sl(_O_KI, D_IDX)
    parts = [sl(_O_AGLU, 2 * D_A), sl(_O_UV, 2 * D_C), sl(_O_G, 3 * D_MODEL), sl(_O_Q, D_ATT),
             sl(_O_QI, N_IDX_HEADS * D_IDX), sl(_O_K, N_KV * HEAD_DIM), sl(_O_V, N_KV * HEAD_DIM),
             ki, ki, ki, ki, sl(_O_WI, N_IDX_HEADS), _pad_cols(D_ZP - C_WI - N_IDX_HEADS, w.dtype, w)]
    return jnp.concatenate(parts, axis=-1)


def kernel(x_prompt, x_sample, cache_k, cache_v, cache_kidx, state_conv, page_table, c_prompt, c_sample,
           w_ada, b_ada, w_in, b_in, conv_k, conv_b, lnA_g, lnA_b, lnC_g, lnC_b, w_s, b_s,
           w_pa, w_pb, w_pc, w_o, ln1_g, ln1_b, ln2_g, ln2_b, rel_table, w_router, router_bias,
           w_e_gate, w_e_up, w_e_down):
    depth = w_in.shape[0]
    alpha = (2 * depth) ** 0.25
    bp, tp, d = x_prompt.shape
    bs, ts, _ = x_sample.shape
    n_pages = page_table.shape[1]
    past = n_pages * PAGE_SIZE
    topk_s = min(TOPK_MAX, (past + ts) // 4)
    n_pool = cache_k.shape[1]

    w_in_p = _relayout_in(w_in).astype(BF16)
    b_in_p = _relayout_in(b_in).reshape(depth, 1, D_ZP)
    w_pa_b, w_pb_b, w_pc_b, w_o_b = (w.astype(BF16) for w in (w_pa, w_pb, w_pc, w_o))
    w_g_b, w_u_b, w_d_b = (w.astype(BF16) for w in (w_e_gate, w_e_up, w_e_down))
    wr_t = w_router.T
    wr_hi = wr_t.astype(BF16)
    wr_lo = (wr_t - wr_hi.astype(F32)).astype(BF16)
    rbias = router_bias.reshape(N_EXPERTS, 1)
    grp_w = D_C // N_CGROUPS
    bs_tab_p = jnp.repeat(jnp.swapaxes(b_s[:, :, :CHUNK], 1, 2), grp_w, axis=2)
    bs_tab_s = bs_tab_p[:, :ts]
    wl_tab = jnp.repeat(jnp.transpose(w_s[:, :, :ts, :ts], (0, 3, 2, 1)), grp_w, axis=3)
    cache_k2 = jnp.transpose(cache_k, (0, 1, 3, 4, 2)).reshape(depth, n_pool, N_KV * HEAD_DIM, PAGE_SIZE)
    cache_v2 = jnp.transpose(cache_v, (0, 1, 3, 4, 2)).reshape(depth, n_pool, N_KV * HEAD_DIM, PAGE_SIZE)
    cache_ki2 = jnp.transpose(cache_kidx, (0, 1, 3, 2))

    def row(a, l):
        return a[l].reshape(1, -1)

    bias_p = _bias_near(rel_table)
    bias_s = bias_p.reshape(N_KV, GROUP, KEY_BLK, 2 * KEY_BLK)[:, :, :ts].reshape(N_KV, GROUP * ts, 2 * KEY_BLK)

    n_c = bp + bs
    c_all = jnp.concatenate([c_prompt, c_sample, jnp.zeros((-n_c % 8, d), F32)], axis=0)
    mod_all = _ada(c_all, w_ada, b_ada)

    yp, ys = x_prompt, x_sample
    outs = {k: [] for k in ("kp", "vp", "kip", "cp", "ks", "vs", "kis", "cs", "vcs")}
    rb_p = 512
    for l in range(depth):
        mod_p = mod_all[l, :bp].reshape(bp, 1, 6 * d)
        mod_s = mod_all[l, bp:n_c].reshape(bs, 1, 6 * d)
        lnA = (row(lnA_g, l), row(lnA_b, l))
        lnC = (row(lnC_g, l), row(lnC_b, l))
        ln1 = (row(ln1_g, l), row(ln1_b, l))
        ln2 = (row(ln2_g, l), row(ln2_b, l))

        z = _inproj(yp, mod_p, w_in_p[l], b_in_p[l], 1, rb_p, 3328)
        ya, conv_state = _conv_prompt(z, conv_k[l], row(conv_b, l), *lnA)
        yb = _dsa_prompt(z, bias_p)
        yc = _sgu_prompt(z, *lnC, w_s[l], bs_tab_p[l])
        x1, h2, gate_t = _merge(ya, yb, yc, z, yp, mod_p, w_pa_b[l], w_pb_b[l], w_pc_b[l], w_o_b[l], *ln1,
                                wr_hi, wr_lo, rbias, 1, rb_p, BF16, alpha)
        yp = _moe(h2, gate_t.T, x1, mod_p, w_g_b, w_u_b, w_d_b, *ln2, l, 1, 1024, alpha)
        outs["kp"].append(z[:, :, C_K:C_K + N_KV * HEAD_DIM].reshape(bp, tp, N_KV, HEAD_DIM))
        outs["vp"].append(z[:, :, C_VV:C_VV + N_KV * HEAD_DIM].reshape(bp, tp, N_KV, HEAD_DIM))
        outs["kip"].append(z[:, :, C_KI4:C_KI4 + D_IDX])
        outs["cp"].append(conv_state)

        zs = _inproj(ys, mod_s, w_in_p[l], b_in_p[l], bs, ts, 1664)
        ya, conv_state = _conv_sample(zs, state_conv[l], conv_k[l], row(conv_b, l), *lnA)
        keys = _dsa_sample_scores(zs, cache_ki2, page_table, l)
        thr, cut = _dsa_sample_topk(keys, topk_s)
        yb = _dsa_sample_attn(zs, keys, thr, cut, cache_k2, cache_v2, page_table, l, bias_s)
        yc, v_chunk = _sgu_sample(zs, *lnC, wl_tab[l], bs_tab_s[l])
        x1, h2, gate_t = _merge(ya, yb, yc, zs, ys, mod_s, w_pa_b[l], w_pb_b[l], w_pc_b[l], w_o_b[l], *ln1,
                                wr_hi, wr_lo, rbias, bs, ts, F32, alpha)
        ys = _moe(h2, gate_t.T, x1, mod_s, w_g_b, w_u_b, w_d_b, *ln2, l, bs, ts, alpha)
        outs["ks"].append(zs[:, :, C_K:C_K + N_KV * HEAD_DIM].reshape(bs, ts, N_KV, HEAD_DIM))
        outs["vs"].append(zs[:, :, C_VV:C_VV + N_KV * HEAD_DIM].reshape(bs, ts, N_KV, HEAD_DIM))
        outs["kis"].append(zs[:, :, C_KI4:C_KI4 + D_IDX])
        outs["cs"].append(conv_state)
        outs["vcs"].append(v_chunk)

    st = {k: jnp.stack(v) for k, v in outs.items()}
    return (yp, ys, st["kp"], st["vp"], st["kip"], st["cp"],
            st["ks"], st["vs"], st["kis"], st["cs"], st["vcs"])
```

```python
import functools
import math

import jax
import jax.numpy as jnp
from jax import lax
from jax.experimental import pallas as pl
from jax.experimental.pallas import tpu as pltpu

F32 = jnp.float32
BF16 = jnp.bfloat16
I32 = jnp.int32

D_MODEL = 1024
D_A = 512
CONV_W = 31
N_HEADS = 8
N_KV = 2
GROUP = N_HEADS // N_KV
HEAD_DIM = 64
D_ATT = N_HEADS * HEAD_DIM
N_IDX_HEADS = 4
D_IDX = 64
TOPK_MAX = 256
N_BUCKETS = 32
MAX_DIST = 128
D_C = 512
N_CGROUPS = 8
CHUNK = 128
N_EXPERTS = 16
N_GROUPS = 4
EXPERTS_PER_GROUP = N_EXPERTS // N_GROUPS
D_EXPERT = 512
PAGE_SIZE = 128
LN_EPS = 1e-5

LANES = 128
KEY_BLK = 128
FAR_BLKS = 4
Q_PAIR = 4
NEG_BIG = -1e30
LOG2E = math.log2(math.e)
INT_MIN = -2 ** 31
VMEM_LIMIT = 56 * 1024 * 1024

C_A, C_AG, C_U, C_V = 0, 512, 1024, 1536
C_GA, C_GB, C_GC = 2048, 3072, 4096
C_Q, C_QI, C_K, C_VV, C_KI4, C_WI = 5120, 5632, 5888, 6016, 6144, 6400
D_ZP = 6656

_O_AGLU, _O_Q, _O_K, _O_V, _O_QI, _O_KI, _O_WI, _O_UV, _O_G = 0, 1024, 1536, 1664, 1792, 2048, 2112, 2116, 3140
_D_IN = 6212


def _cp(n_axes):
    return pltpu.CompilerParams(dimension_semantics=("arbitrary",) * n_axes, vmem_limit_bytes=VMEM_LIMIT)


def _sigmoid(x):
    return jax.nn.sigmoid(x)


def _ln(x, g, b):
    mu = jnp.mean(x, axis=-1, keepdims=True)
    xc = x - mu
    var = jnp.mean(xc * xc, axis=-1, keepdims=True)
    return xc * lax.rsqrt(var + LN_EPS) * g + b


def _dot(a, b):
    return jnp.dot(a, b, preferred_element_type=F32)


def _dot_nt(a, b):
    return lax.dot_general(a, b, (((1,), (1,)), ((), ())), preferred_element_type=F32)


def _ada_kernel(c_ref, w_ref, b_ref, o_ref):
    c = c_ref[...]
    s = (c * _sigmoid(c)).astype(BF16)
    o_ref[0] = _dot(s, w_ref[0].astype(BF16)) + b_ref[0]


def _ada(c_all, w_ada, b_ada):
    depth, d, n = w_ada.shape
    m = c_all.shape[0]
    tn = 1536
    return pl.pallas_call(
        _ada_kernel,
        grid=(depth, n // tn),
        in_specs=[pl.BlockSpec((m, d), lambda l, j: (0, 0)),
                  pl.BlockSpec((1, d, tn), lambda l, j: (l, 0, j)),
                  pl.BlockSpec((1, 1, tn), lambda l, j: (l, 0, j))],
        out_specs=pl.BlockSpec((1, m, tn), lambda l, j: (l, 0, j)),
        out_shape=jax.ShapeDtypeStruct((depth, m, n), F32),
        compiler_params=_cp(2),
        name="ada",
    )(c_all, w_ada, b_ada.reshape(depth, 1, n))


def _inproj_kernel(x_ref, sh_ref, sc_ref, w_ref, b_ref, z_ref):
    gb, rb, d = x_ref.shape
    h = x_ref[...] * (1.0 + sc_ref[...]) + sh_ref[...]
    z = _dot(h.reshape(gb * rb, d).astype(BF16), w_ref[...]) + b_ref[...]
    z_ref[...] = z.reshape(gb, rb, z.shape[-1])


def _inproj(x3, mod, w, b, gb, rb, tn):
    g, r, d = x3.shape
    n = w.shape[1]
    return pl.pallas_call(
        _inproj_kernel,
        grid=(n // tn, g // gb, r // rb),
        in_specs=[pl.BlockSpec((gb, rb, d), lambda j, gi, ri: (gi, ri, 0)),
                  pl.BlockSpec((gb, 1, d), lambda j, gi, ri: (gi, 0, 0)),
                  pl.BlockSpec((gb, 1, d), lambda j, gi, ri: (gi, 0, 1)),
                  pl.BlockSpec((d, tn), lambda j, gi, ri: (0, j)),
                  pl.BlockSpec((1, tn), lambda j, gi, ri: (0, j))],
        out_specs=pl.BlockSpec((gb, rb, tn), lambda j, gi, ri: (gi, ri, j)),
        out_shape=jax.ShapeDtypeStruct((g, r, n), F32),
        compiler_params=_cp(3),
        name="inproj",
    )(x3, mod, mod, w, b)


CONV_HALO = 32
CONV_ROWS = 32


SUBLANES = 8


def _conv_prompt_kernel(a_ref, g_ref, ah_ref, gh_ref, ck_ref, cb_ref, lg_ref, lb_ref, y_ref, st_ref,
                        ext_scr, sh_scr):
    t = pl.program_id(1)
    tt = a_ref.shape[1]
    a = a_ref[0] * _sigmoid(g_ref[0])
    halo = ah_ref[0] * _sigmoid(gh_ref[0])
    ext_scr[0:CONV_HALO, :] = jnp.where(t > 0, halo, 0.0)
    ext_scr[CONV_HALO:, :] = a
    off = CONV_HALO - (CONV_W - 1)
    ext_rows = tt + CONV_HALO
    for ph in range(SUBLANES):
        n_rows = ext_rows if ph == 0 else ext_rows - SUBLANES
        sh_scr[ph, 0:n_rows, :] = ext_scr[ph:ph + n_rows, :]
    for c in range(tt // CONV_ROWS):
        r0 = c * CONV_ROWS
        acc = jnp.zeros((CONV_ROWS, D_A), F32) + cb_ref[...]
        for j in range(CONV_W):
            ph, base = (off + j) % SUBLANES, (off + j) // SUBLANES * SUBLANES
            acc = acc + ck_ref[j:j + 1, :] * sh_scr[ph, r0 + base:r0 + base + CONV_ROWS, :]
        y = _ln(acc, lg_ref[...], lb_ref[...])
        y_ref[0, r0:r0 + CONV_ROWS, :] = (y * _sigmoid(y)).astype(y_ref.dtype)

    @pl.when(t == pl.num_programs(1) - 1)
    def _():
        st_ref[0] = ext_scr[tt + off:tt + CONV_HALO, :]


def _conv_prompt(z, ck, cb, lg, lb, tt=256):
    b, t, _ = z.shape
    hb = tt // CONV_HALO
    return pl.pallas_call(
        _conv_prompt_kernel,
        grid=(b, t // tt),
        in_specs=[pl.BlockSpec((1, tt, D_A), lambda bi, ti: (bi, ti, C_A // D_A)),
                  pl.BlockSpec((1, tt, D_A), lambda bi, ti: (bi, ti, C_AG // D_A)),
                  pl.BlockSpec((1, CONV_HALO, D_A), lambda bi, ti: (bi, jnp.maximum(ti * hb - 1, 0), C_A // D_A)),
                  pl.BlockSpec((1, CONV_HALO, D_A), lambda bi, ti: (bi, jnp.maximum(ti * hb - 1, 0), C_AG // D_A)),
                  pl.BlockSpec((CONV_W, D_A), lambda bi, ti: (0, 0)),
                  pl.BlockSpec((1, D_A), lambda bi, ti: (0, 0)),
                  pl.BlockSpec((1, D_A), lambda bi, ti: (0, 0)),
                  pl.BlockSpec((1, D_A), lambda bi, ti: (0, 0))],
        out_specs=[pl.BlockSpec((1, tt, D_A), lambda bi, ti: (bi, ti, 0)),
                   pl.BlockSpec((1, CONV_W - 1, D_A), lambda bi, ti: (bi, 0, 0))],
        out_shape=[jax.ShapeDtypeStruct((b, t, D_A), BF16),
                   jax.ShapeDtypeStruct((b, CONV_W - 1, D_A), F32)],
        scratch_shapes=[pltpu.VMEM((tt + CONV_HALO, D_A), F32),
                        pltpu.VMEM((SUBLANES, tt + CONV_HALO, D_A), F32)],
        compiler_params=_cp(2),
        name="conv_prompt",
    )(z, z, z, z, ck, cb, lg, lb)


def _conv_sample_kernel(a_ref, g_ref, st_ref, ck_ref, cb_ref, lg_ref, lb_ref, y_ref, ns_ref, ext_scr):
    bs, t, _ = a_ref.shape
    hist = CONV_W - 1
    ext_scr[:, 0:hist, :] = st_ref[...]
    ext_scr[:, hist:hist + t, :] = a_ref[...] * _sigmoid(g_ref[...])
    ns_ref[...] = ext_scr[:, t:t + hist, :]

    def body(b, carry):
        acc = jnp.zeros((t, D_A), F32) + cb_ref[...]
        for j in range(CONV_W):
            acc = acc + ck_ref[j:j + 1, :] * ext_scr[b, j:j + t, :]
        y = _ln(acc, lg_ref[...], lb_ref[...])
        y_ref[b] = y * _sigmoid(y)
        return carry

    lax.fori_loop(0, bs, body, 0)


def _conv_sample(z, state, ck, cb, lg, lb, bs=32):
    b, t, _ = z.shape
    hist = CONV_W - 1
    bs = min(bs, b)
    return pl.pallas_call(
        _conv_sample_kernel,
        grid=(b // bs,),
        in_specs=[pl.BlockSpec((bs, t, D_A), lambda bi: (bi, 0, C_A // D_A)),
                  pl.BlockSpec((bs, t, D_A), lambda bi: (bi, 0, C_AG // D_A)),
                  pl.BlockSpec((bs, hist, D_A), lambda bi: (bi, 0, 0)),
                  pl.BlockSpec((CONV_W, D_A), lambda bi: (0, 0)),
                  pl.BlockSpec((1, D_A), lambda bi: (0, 0)),
                  pl.BlockSpec((1, D_A), lambda bi: (0, 0)),
                  pl.BlockSpec((1, D_A), lambda bi: (0, 0))],
        out_specs=[pl.BlockSpec((bs, t, D_A), lambda bi: (bi, 0, 0)),
                   pl.BlockSpec((bs, hist, D_A), lambda bi: (bi, 0, 0))],
        out_shape=[jax.ShapeDtypeStruct((b, t, D_A), F32),
                   jax.ShapeDtypeStruct((b, hist, D_A), F32)],
        scratch_shapes=[pltpu.VMEM((bs, hist + t + 2, D_A), F32)],
        compiler_params=_cp(1),
        name="conv_sample",
    )(z, z, state, ck, cb, lg, lb)


def _sgu_prompt_kernel(u_ref, v_ref, lg_ref, lb_ref, ws_ref, bs_ref, y_ref):
    tt = u_ref.shape[1]
    vv = _ln(v_ref[0], lg_ref[...], lb_ref[...])
    row = lax.broadcasted_iota(I32, (CHUNK, CHUNK), 0)
    col = lax.broadcasted_iota(I32, (CHUNK, CHUNK), 1)
    lane_grp = lax.broadcasted_iota(I32, (CHUNK, D_C), 1) // (D_C // N_CGROUPS)
    ws = [jnp.where(col <= row, ws_ref[g], 0.0).astype(BF16) for g in range(N_CGROUPS)]
    for c in range(tt // CHUNK):
        vc = vv[c * CHUNK:(c + 1) * CHUNK].astype(BF16)
        mixed = bs_ref[...]
        for g in range(N_CGROUPS):
            mixed = mixed + _dot(ws[g], jnp.where(lane_grp == g, vc, jnp.zeros_like(vc)))
        y_ref[0, c * CHUNK:(c + 1) * CHUNK, :] = (u_ref[0, c * CHUNK:(c + 1) * CHUNK, :] * mixed).astype(y_ref.dtype)


def _sgu_prompt(z, lg, lb, w_s, bs_tab, tt=512):
    b, t, _ = z.shape
    return pl.pallas_call(
        _sgu_prompt_kernel,
        grid=(b, t // tt),
        in_specs=[pl.BlockSpec((1, tt, D_C), lambda bi, ti: (bi, ti, C_U // D_C)),
                  pl.BlockSpec((1, tt, D_C), lambda bi, ti: (bi, ti, C_V // D_C)),
                  pl.BlockSpec((1, D_C), lambda bi, ti: (0, 0)),
                  pl.BlockSpec((1, D_C), lambda bi, ti: (0, 0)),
                  pl.BlockSpec((N_CGROUPS, CHUNK, CHUNK), lambda bi, ti: (0, 0, 0)),
                  pl.BlockSpec((CHUNK, D_C), lambda bi, ti: (0, 0))],
        out_specs=pl.BlockSpec((1, tt, D_C), lambda bi, ti: (bi, ti, 0)),
        out_shape=jax.ShapeDtypeStruct((b, t, D_C), BF16),
        compiler_params=_cp(2),
        name="sgu_prompt",
    )(z, z, lg, lb, w_s, bs_tab)


def _sgu_sample_kernel(u_ref, v_ref, lg_ref, lb_ref, wl_ref, bs_ref, y_ref, vc_ref):
    bs, t, _ = u_ref.shape
    vv = _ln(v_ref[...], lg_ref[...], lb_ref[...])
    vc_ref[...] = vv
    tpos = lax.broadcasted_iota(I32, (t, D_C), 0)
    mixed = jnp.zeros((bs, t, D_C), F32) + bs_ref[...]
    for s in range(t):
        w = jnp.where(tpos >= s, wl_ref[s], 0.0)
        mixed = mixed + w * vv[:, s:s + 1, :]
    y_ref[...] = u_ref[...] * mixed


def _sgu_sample(z, lg, lb, wl_tab, bs_tab):
    b, t, _ = z.shape
    return pl.pallas_call(
        _sgu_sample_kernel,
        grid=(1,),
        in_specs=[pl.BlockSpec((b, t, D_C), lambda i: (0, 0, C_U // D_C)),
                  pl.BlockSpec((b, t, D_C), lambda i: (0, 0, C_V // D_C)),
                  pl.BlockSpec((1, D_C), lambda i: (0, 0)),
                  pl.BlockSpec((1, D_C), lambda i: (0, 0)),
                  pl.BlockSpec((t, t, D_C), lambda i: (0, 0, 0)),
                  pl.BlockSpec((t, D_C), lambda i: (0, 0))],
        out_specs=[pl.BlockSpec((b, t, D_C), lambda i: (0, 0, 0)),
                   pl.BlockSpec((b, t, D_C), lambda i: (0, 0, 0))],
        out_shape=[jax.ShapeDtypeStruct((b, t, D_C), F32),
                   jax.ShapeDtypeStruct((b, t, D_C), F32)],
        compiler_params=_cp(1),
        name="sgu_sample",
    )(z, z, lg, lb, wl_tab, bs_tab)


def _bias_kernel(tab_ref, o_ref):
    rows, cols = KEY_BLK, 2 * KEY_BLK
    i = lax.broadcasted_iota(I32, (rows, cols), 0)
    j = lax.broadcasted_iota(I32, (rows, cols), 1)
    n = jnp.maximum(KEY_BLK + i - j, 0)
    max_exact = N_BUCKETS // 2
    nf = jnp.maximum(n, 1).astype(F32)
    large = max_exact + (jnp.log(nf / max_exact) / math.log(MAX_DIST / max_exact) * (N_BUCKETS - max_exact)).astype(I32)
    large = jnp.minimum(large, N_BUCKETS - 1)
    bucket = jnp.where(n < max_exact, n, large)
    for h in range(N_HEADS):
        acc = jnp.zeros((rows, cols), F32)
        for bkt in range(N_BUCKETS):
            acc = jnp.where(bucket == bkt, tab_ref[bkt, h], acc)
        o_ref[h // GROUP, (h % GROUP) * rows:(h % GROUP + 1) * rows, :] = (acc - tab_ref[N_BUCKETS - 1, h]) * LOG2E


def _bias_near(rel_table):
    return pl.pallas_call(
        _bias_kernel,
        grid=(1,),
        in_specs=[pl.BlockSpec(memory_space=pltpu.SMEM)],
        out_specs=pl.BlockSpec((N_KV, GROUP * KEY_BLK, 2 * KEY_BLK), lambda i: (0, 0, 0)),
        out_shape=jax.ShapeDtypeStruct((N_KV, GROUP * KEY_BLK, 2 * KEY_BLK), F32),
        compiler_params=_cp(1),
        name="bias_near",
    )(rel_table)


def _score_keys(s):
    s = jnp.where(s == 0.0, 0.0, s)
    bits = lax.bitcast_convert_type(s, I32)
    return bits ^ (jnp.right_shift(bits, 31) & 0x7FFFFFFF)


def _lane_top2(s, m1, m2):
    for blk in [s[:, c * LANES:(c + 1) * LANES] for c in range(s.shape[1] // LANES)]:
        m2 = jnp.maximum(m2, jnp.minimum(m1, blk))
        m1 = jnp.maximum(m1, blk)
    return m1, m2


def _topk_threshold(load_blk, n_iter, unroll, rows, k, idx_bits, cut_scr, bounds=None, parts=1):
    lane = lax.broadcasted_iota(I32, (rows, KEY_BLK), 1)
    groups = range(parts)
    if bounds is None:
        first_bit = 0
        thr0 = [jnp.full((rows, 1), INT_MIN, I32) for _ in groups]
    else:
        shared = [jnp.min(lax.clz(lo ^ hi)) for lo, hi in bounds]
        first_bit = jnp.minimum(functools.reduce(jnp.minimum, shared), 31)
        keep = ~(jnp.left_shift(jnp.int32(2), 31 - first_bit) - 1)
        thr0 = [((hi ^ INT_MIN) & keep) ^ INT_MIN for _, hi in bounds]

    def count(preds):
        accs = []
        for p in groups:
            def body(jj, acc, p=p):
                for u in range(unroll):
                    j = jj * unroll + u
                    acc = jnp.where(preds[p](load_blk(p, j), j), acc + 1.0, acc)
                return acc
            accs.append(lax.fori_loop(0, n_iter, body, jnp.zeros((rows, KEY_BLK), F32)))
        return [jnp.sum(a, axis=1, keepdims=True) for a in accs]

    def ge(xs):
        return [lambda kb, j, x=x: kb >= x for x in xs]

    kf = float(k)

    def bit_body(i, thr):
        cand = [t ^ jnp.left_shift(jnp.int32(1), 31 - i) for t in thr]
        cnt = count(ge(cand))
        return [jnp.where(cnt[p] >= kf, cand[p], thr[p]) for p in groups]

    thr = lax.fori_loop(first_bit, 32, bit_body, thr0)
    cnt_gt = count([lambda kb, j, t=t: kb > t for t in thr])
    need = [kf - c for c in cnt_gt]
    if cut_scr is None:
        return thr, need
    assert parts == 1
    thr, need = thr[0], need[0]
    cnt_ge = count(ge([thr]))[0]
    cut_scr[...] = jnp.full((rows, 1), 2 ** idx_bits, I32)

    @pl.when(jnp.max(cnt_ge) > kf)
    def _():
        def idx_body(i, r):
            cand = r + jnp.left_shift(jnp.int32(1), idx_bits - 1 - i)
            cnt = count([lambda kb, j: (kb == thr) & (j * KEY_BLK + lane < cand)])[0]
            return jnp.where(cnt < need, cand, r)
        r = lax.fori_loop(0, idx_bits, idx_body, jnp.zeros((rows, 1), I32))
        cut_scr[...] = r + 1

    return thr, cut_scr[...]


def _select_mask(keys, thr, cut, gidx):
    return (keys > thr) | ((keys == thr) & (gidx < cut))


def _stack_heads(q, n, rows):
    half = lax.broadcasted_iota(I32, (rows, LANES), 1) // HEAD_DIM
    tiles = []
    for g in range(GROUP):
        h = n * GROUP + g
        col = q[:, (h // 2) * LANES:(h // 2 + 1) * LANES]
        if h % 2 != n:
            col = pltpu.roll(col, HEAD_DIM, 1)
        tiles.append(jnp.where(half == n, col, 0.0))
    return jnp.concatenate(tiles, axis=0)


def _unstack_heads(o_groups, rows):
    half = lax.broadcasted_iota(I32, (rows, LANES), 1) // HEAD_DIM
    cols = []
    for c in range(N_HEADS // 2):
        parts = []
        for h in (2 * c, 2 * c + 1):
            n, g = h // GROUP, h % GROUP
            tile = o_groups[n][g * rows:(g + 1) * rows, :]
            if h % 2 != n:
                tile = pltpu.roll(tile, HEAD_DIM, 1)
            parts.append(tile)
        cols.append(jnp.where(half == 0, parts[0], parts[1]))
    return jnp.concatenate(cols, axis=1)


def _dsa_prompt_kernel(q_ref, qi_ref, wi_ref, ki_ref, k_ref, v_ref, bias_ref, o_ref,
                       key_scr, qs_scr, mpart_scr, mfull_scr, lpart_scr, acc_scr, p_scr,
                       lgfar_scr, lgnear_scr, *, topk):
    pair = pl.program_id(1)
    rows = KEY_BLK
    chunk = FAR_BLKS * KEY_BLK
    row_i = lax.broadcasted_iota(I32, (rows, 1), 0)
    lane_c = lax.broadcasted_iota(I32, (rows, chunk), 1)
    n_chunks = (pair * Q_PAIR) // FAR_BLKS + 1
    assert FAR_BLKS % Q_PAIR == 0

    def scores(s):
        sub = slice(s * rows, (s + 1) * rows)
        tpos = (pair * Q_PAIR + s) * rows + row_i
        qi = qi_ref[0, sub, :]
        lane_head = lax.broadcasted_iota(I32, qi.shape, 1) // D_IDX
        qim = [jnp.where(lane_head == h, qi, 0.0).astype(BF16) for h in range(N_IDX_HEADS)]
        wi = wi_ref[0, sub, :]

        def score_body(c, top2):
            kc = ki_ref[0, pl.ds(pl.multiple_of(c * chunk, chunk), chunk), :].astype(BF16)
            sc = jnp.zeros((rows, chunk), F32)
            for h in range(N_IDX_HEADS):
                sc = sc + wi[:, h:h + 1] * jnp.maximum(_dot_nt(qim[h], kc), 0.0)
            sc = jnp.where(c * chunk + lane_c <= tpos, sc, -jnp.inf)
            keys = _score_keys(sc)
            for j in range(FAR_BLKS):
                key_scr[s, c * FAR_BLKS + j] = keys[:, j * KEY_BLK:(j + 1) * KEY_BLK]
            return _lane_top2(sc, *top2)

        ninf = jnp.full((rows, LANES), -jnp.inf, F32)
        m1, m2 = lax.fori_loop(0, n_chunks, score_body, (ninf, ninf))
        return (_score_keys(jnp.min(m2, axis=1, keepdims=True)), _score_keys(jnp.max(m1, axis=1, keepdims=True)))

    assert topk <= 2 * LANES
    bounds = [scores(s) for s in range(Q_PAIR)]
    thr_all, need_all = _topk_threshold(lambda s, j: key_scr[s, j], n_chunks, FAR_BLKS, rows, topk, 0, None,
                                        bounds, parts=Q_PAIR)

    for s in range(Q_PAIR):
        sub = slice(s * rows, (s + 1) * rows)
        _dsa_prompt_attend(pair * Q_PAIR + s, thr_all[s], need_all[s], q_ref.at[0, sub, :], k_ref, v_ref,
                           bias_ref, o_ref.at[0, sub, :], key_scr.at[s], qs_scr, mpart_scr, mfull_scr, lpart_scr,
                           acc_scr, p_scr, lgfar_scr, lgnear_scr)


def _dsa_prompt_attend(qb, thr, need, q_ref, k_ref, v_ref, bias_ref, o_ref, key_scr, qs_scr, mpart_scr, mfull_scr,
                       lpart_scr, acc_scr, p_scr, lgfar_scr, lgnear_scr):
    rows = KEY_BLK
    chunk = FAR_BLKS * KEY_BLK
    tpos = qb * rows + lax.broadcasted_iota(I32, (rows, 1), 0)

    ri = lax.broadcasted_iota(I32, (KEY_BLK, KEY_BLK), 0)
    ci = lax.broadcasted_iota(I32, (KEY_BLK, KEY_BLK), 1)
    prefix_mat = jnp.where(ri <= ci, 1.0, 0.0).astype(BF16)
    ones_mat = jnp.ones((KEY_BLK, KEY_BLK), BF16)

    def select_block(kb, run):
        eq = kb == thr
        e = jnp.where(eq, 1.0, 0.0).astype(BF16)
        sel = (kb > thr) | (eq & (run + _dot(e, prefix_mat) <= need))
        return sel, run + _dot(e, ones_mat)

    q = q_ref[...] * (HEAD_DIM ** -0.5 * LOG2E)
    for n in range(N_KV):
        qs_scr[n] = _stack_heads(q, n, rows).astype(BF16)
    mpart_scr[...] = jnp.full(mpart_scr.shape, NEG_BIG, F32)
    lpart_scr[...] = jnp.zeros(lpart_scr.shape, F32)
    acc_scr[...] = jnp.zeros(acc_scr.shape, F32)

    def lane_blocks(t):
        return [t[:, c * LANES:(c + 1) * LANES] for c in range(t.shape[1] // LANES)]

    def fold(vals, op):
        while len(vals) > 1:
            vals = [op(vals[i], vals[i + 1]) for i in range(0, len(vals), 2)]
        return vals[0]

    def sweep_max(kc, neg, with_bias, lg_ref):
        for n in range(N_KV):
            lg = _dot_nt(qs_scr[n], kc)
            for g in range(GROUP):
                sl = slice(g * rows, (g + 1) * rows)
                t = lg[sl] + neg
                if with_bias:
                    t = t + bias_ref[n, sl, :]
                lg_ref[n, sl, :] = t
                mpart_scr[n, sl, :] = jnp.maximum(mpart_scr[n, sl, :], fold(lane_blocks(t), jnp.maximum))

    def sweep_acc(vc, lg_ref):
        width = vc.shape[0]
        for n in range(N_KV):
            for g in range(GROUP):
                sl = slice(g * rows, (g + 1) * rows)
                mf = mfull_scr[n, sl, :]
                ps = [jnp.exp2(tb - mf) for tb in lane_blocks(lg_ref[n, sl, :])]
                lpart_scr[n, sl, :] = lpart_scr[n, sl, :] + fold(ps, jnp.add)
                p_scr[sl, 0:width] = jnp.concatenate(ps, axis=1).astype(BF16)
            acc_scr[n] = acc_scr[n] + _dot(p_scr[:, 0:width], vc)

    n_far = (qb + 2) // FAR_BLKS

    def far_max(c, run):
        start = pl.multiple_of(c * chunk, chunk)
        kc = k_ref[0, pl.ds(start, chunk), :].astype(BF16)
        negs = []
        for j in range(FAR_BLKS):
            blk = c * FAR_BLKS + j
            is_far = blk < qb - 1
            sel, run_next = select_block(key_scr[blk], run)
            run = jnp.where(is_far, run_next, run)
            negs.append(jnp.where(sel & is_far, 0.0, NEG_BIG))
        sweep_max(kc, jnp.concatenate(negs, axis=1), False, lgfar_scr.at[c])
        return run

    run = lax.fori_loop(0, n_far, far_max, jnp.zeros((rows, KEY_BLK), F32))

    pb = jnp.maximum(qb - 1, 0)
    p0 = pl.multiple_of(pb * rows, rows)
    q0 = pl.multiple_of(qb * rows, rows)
    lane_b = lax.broadcasted_iota(I32, (rows, KEY_BLK), 1)
    sel_prev, run_next = select_block(key_scr[pb], run)
    run = jnp.where(qb > 0, run_next, run)
    sel_prev = sel_prev & (qb > 0)
    sel_diag, _ = select_block(key_scr[qb], run)
    sel_diag = sel_diag & (qb * rows + lane_b <= tpos)
    neg_near = jnp.concatenate([jnp.where(sel_prev, 0.0, NEG_BIG), jnp.where(sel_diag, 0.0, NEG_BIG)], axis=1)

    def near_kv(ref):
        return jnp.concatenate([ref[0, pl.ds(p0, rows), :], ref[0, pl.ds(q0, rows), :]], axis=0).astype(BF16)

    sweep_max(near_kv(k_ref), neg_near, True, lgnear_scr)
    for n in range(N_KV):
        mfull_scr[n] = jnp.broadcast_to(jnp.max(mpart_scr[n], axis=1, keepdims=True), mfull_scr.shape[1:])

    def far_acc(c, carry):
        start = pl.multiple_of(c * chunk, chunk)
        sweep_acc(v_ref[0, pl.ds(start, chunk), :].astype(BF16), lgfar_scr.at[c])
        return carry

    lax.fori_loop(0, n_far, far_acc, 0)
    sweep_acc(near_kv(v_ref), lgnear_scr)

    outs = [acc_scr[n] / jnp.sum(lpart_scr[n], axis=1, keepdims=True) for n in range(N_KV)]
    o_ref[...] = _unstack_heads(outs, rows).astype(o_ref.dtype)


def _dsa_prompt(z, bias_near):
    b, t, _ = z.shape
    nb = t // KEY_BLK
    assert nb % FAR_BLKS == 0
    topk = min(TOPK_MAX, t // 4)
    grows = GROUP * KEY_BLK
    chunk = FAR_BLKS * KEY_BLK
    qrows = Q_PAIR * KEY_BLK
    return pl.pallas_call(
        functools.partial(_dsa_prompt_kernel, topk=topk),
        grid=(b, nb // Q_PAIR),
        in_specs=[pl.BlockSpec((1, qrows, D_ATT), lambda bi, qi: (bi, qi, C_Q // D_ATT)),
                  pl.BlockSpec((1, qrows, 256), lambda bi, qi: (bi, qi, C_QI // 256)),
                  pl.BlockSpec((1, qrows, LANES), lambda bi, qi: (bi, qi, C_WI // LANES)),
                  pl.BlockSpec((1, t, 256), lambda bi, qi: (bi, 0, C_KI4 // 256)),
                  pl.BlockSpec((1, t, LANES), lambda bi, qi: (bi, 0, C_K // LANES)),
                  pl.BlockSpec((1, t, LANES), lambda bi, qi: (bi, 0, C_VV // LANES)),
                  pl.BlockSpec((N_KV, grows, 2 * KEY_BLK), lambda bi, qi: (0, 0, 0))],
        out_specs=pl.BlockSpec((1, qrows, D_ATT), lambda bi, qi: (bi, qi, 0)),
        out_shape=jax.ShapeDtypeStruct((b, t, D_ATT), BF16),
        scratch_shapes=[pltpu.VMEM((Q_PAIR, nb, KEY_BLK, KEY_BLK), I32),
                        pltpu.VMEM((N_KV, grows, LANES), BF16),
                        pltpu.VMEM((N_KV, grows, LANES), F32),
                        pltpu.VMEM((N_KV, grows, LANES), F32),
                        pltpu.VMEM((N_KV, grows, LANES), F32),
                        pltpu.VMEM((N_KV, grows, LANES), F32),
                        pltpu.VMEM((grows, chunk), BF16),
                        pltpu.VMEM((nb // FAR_BLKS, N_KV, grows, chunk), F32),
                        pltpu.VMEM((N_KV, grows, 2 * KEY_BLK), F32)],
        compiler_params=_cp(2),
        name="dsa_prompt",
    )(z, z, z, z, z, z, bias_near)


def _dsa_sample_score_kernel(pt_ref, qi_ref, wi_ref, kin_ref, *rest):
    n_pages = len(rest) - 1
    pages, o_ref = rest[:n_pages], rest[n_pages]
    t = qi_ref.shape[1]
    qi = qi_ref[0]
    wi = wi_ref[0]
    qs = jnp.concatenate([qi[:, h * D_IDX:(h + 1) * D_IDX] for h in range(N_IDX_HEADS)], axis=0).astype(BF16)
    k_past_t = jnp.concatenate([pg[0, 0] for pg in pages], axis=1).astype(BF16)
    k_new = jnp.concatenate([kin_ref[0][:, :D_IDX], jnp.zeros((KEY_BLK - t, D_IDX), F32)], axis=0).astype(BF16)

    def score(d):
        s = jnp.zeros((t, d.shape[1]), F32)
        for h in range(N_IDX_HEADS):
            s = s + wi[:, h:h + 1] * jnp.maximum(d[h * t:(h + 1) * t], 0.0)
        return s

    keys_past = _score_keys(score(_dot(qs, k_past_t)))
    for j in range(n_pages):
        o_ref[j] = keys_past[:, j * KEY_BLK:(j + 1) * KEY_BLK]
    s_new = score(_dot_nt(qs, k_new))
    lane = lax.broadcasted_iota(I32, s_new.shape, 1)
    trow = lax.broadcasted_iota(I32, s_new.shape, 0)
    o_ref[n_pages] = _score_keys(jnp.where(lane <= trow, s_new, -jnp.inf))


def _page_specs(n_pages, layer, feat):
    return [pl.BlockSpec((1, 1, feat, PAGE_SIZE), lambda bi, pt, j=j: (layer, pt[bi, j], 0, 0))
            for j in range(n_pages)]


def _dsa_sample_scores(z, cache_kidx, page_table, layer):
    b, t, _ = z.shape
    n_pages = page_table.shape[1]
    nblk = n_pages + 1
    grid_spec = pltpu.PrefetchScalarGridSpec(
        num_scalar_prefetch=1,
        grid=(b,),
        in_specs=[pl.BlockSpec((1, t, 256), lambda bi, pt: (bi, 0, C_QI // 256)),
                  pl.BlockSpec((1, t, LANES), lambda bi, pt: (bi, 0, C_WI // LANES)),
                  pl.BlockSpec((1, t, 256), lambda bi, pt: (bi, 0, C_KI4 // 256))]
        + _page_specs(n_pages, layer, D_IDX),
        out_specs=pl.BlockSpec((nblk, t, KEY_BLK), lambda bi, pt: (0, bi, 0)),
    )
    return pl.pallas_call(
        _dsa_sample_score_kernel,
        grid_spec=grid_spec,
        out_shape=jax.ShapeDtypeStruct((nblk, b * t, KEY_BLK), I32),
        compiler_params=_cp(1),
        name="dsa_sample_scores",
    )(page_table, z, z, z, *([cache_kidx] * n_pages))


def _dsa_sample_topk_kernel(key_ref, thr_ref, cut_ref, cut_scr, *, topk):
    nblk, rows, _ = key_ref.shape
    thr, cut = _topk_threshold(lambda p, j: key_ref[j], 1, nblk, rows, topk, 13, cut_scr)
    thr_ref[...] = thr
    cut_ref[...] = cut


def _dsa_sample_topk(keys, topk, rows=128):
    nblk, m, _ = keys.shape
    return pl.pallas_call(
        functools.partial(_dsa_sample_topk_kernel, topk=topk),
        grid=(m // rows,),
        in_specs=[pl.BlockSpec((nblk, rows, KEY_BLK), lambda i: (0, i, 0))],
        out_specs=[pl.BlockSpec((rows, 1), lambda i: (i, 0)),
                   pl.BlockSpec((rows, 1), lambda i: (i, 0))],
        out_shape=[jax.ShapeDtypeStruct((m, 1), I32), jax.ShapeDtypeStruct((m, 1), I32)],
        scratch_shapes=[pltpu.VMEM((rows, 1), I32)],
        compiler_params=_cp(1),
        name="dsa_sample_topk",
    )(keys)


def _dsa_sample_attn_kernel(pt_ref, q_ref, kn_ref, vn_ref, key_ref, thr_ref, cut_ref, bias_ref, *rest):
    n_pages = (len(rest) - 1) // 2
    kpages, vpages, o_ref = rest[:n_pages], rest[n_pages:2 * n_pages], rest[2 * n_pages]
    t = q_ref.shape[1]
    past = n_pages * PAGE_SIZE
    thr = thr_ref[...]
    cut = cut_ref[...]
    q = q_ref[0] * (HEAD_DIM ** -0.5 * LOG2E)
    trow = lax.broadcasted_iota(I32, (t, 1), 0)

    n_far = n_pages - 1
    kt_far = jnp.concatenate([pg[0, 0] for pg in kpages[:n_far]], axis=1).astype(BF16)
    vt_far = jnp.concatenate([pg[0, 0] for pg in vpages[:n_far]], axis=1).astype(BF16)
    keys_far = jnp.concatenate([key_ref[j] for j in range(n_far)], axis=1)
    g_far = lax.broadcasted_iota(I32, keys_far.shape, 1)
    neg_far = jnp.where(_select_mask(keys_far, thr, cut, g_far), 0.0, NEG_BIG)

    pad = jnp.zeros((KEY_BLK - t, LANES), F32)
    kt_last = kpages[n_far][0, 0].astype(BF16)
    vt_last = vpages[n_far][0, 0].astype(BF16)
    k_new = jnp.concatenate([kn_ref[0], pad], axis=0).astype(BF16)
    v_new = jnp.concatenate([vn_ref[0], pad], axis=0).astype(BF16)
    lane_b = lax.broadcasted_iota(I32, (t, KEY_BLK), 1)
    sel_last = _select_mask(key_ref[n_far], thr, cut, n_far * KEY_BLK + lane_b)
    sel_new = _select_mask(key_ref[n_pages], thr, cut, past + lane_b) & (lane_b <= trow)
    neg_near = jnp.concatenate([jnp.where(sel_last, 0.0, NEG_BIG), jnp.where(sel_new, 0.0, NEG_BIG)], axis=1)

    qs = jnp.concatenate([_stack_heads(q, n, t) for n in range(N_KV)], axis=0).astype(BF16)
    bias = jnp.concatenate([bias_ref[n] for n in range(N_KV)], axis=0)
    lg_far = _dot(qs, kt_far) + jnp.concatenate([neg_far] * N_HEADS, axis=0)
    lg_near = (jnp.concatenate([_dot(qs, kt_last), _dot_nt(qs, k_new)], axis=1) + bias
               + jnp.concatenate([neg_near] * N_HEADS, axis=0))
    m = jnp.maximum(jnp.max(lg_far, axis=1, keepdims=True), jnp.max(lg_near, axis=1, keepdims=True))
    p_far = jnp.exp2(lg_far - m)
    p_near = jnp.exp2(lg_near - m)
    l = jnp.sum(p_far, axis=1, keepdims=True) + jnp.sum(p_near, axis=1, keepdims=True)
    p_near = p_near.astype(BF16)
    o = (_dot_nt(p_far.astype(BF16), vt_far) + _dot_nt(p_near[:, :KEY_BLK], vt_last)
         + _dot(p_near[:, KEY_BLK:], v_new)) / l
    o_ref[0] = _unstack_heads([o[n * GROUP * t:(n + 1) * GROUP * t] for n in range(N_KV)], t)


def _dsa_sample_attn(z, keys, thr, cut, cache_k, cache_v, page_table, layer, bias_s):
    b, t, _ = z.shape
    n_pages = page_table.shape[1]
    nblk = n_pages + 1
    grid_spec = pltpu.PrefetchScalarGridSpec(
        num_scalar_prefetch=1,
        grid=(b,),
        in_specs=[pl.BlockSpec((1, t, D_ATT), lambda bi, pt: (bi, 0, C_Q // D_ATT)),
                  pl.BlockSpec((1, t, LANES), lambda bi, pt: (bi, 0, C_K // LANES)),
                  pl.BlockSpec((1, t, LANES), lambda bi, pt: (bi, 0, C_VV // LANES)),
                  pl.BlockSpec((nblk, t, KEY_BLK), lambda bi, pt: (0, bi, 0)),
                  pl.BlockSpec((t, 1), lambda bi, pt: (bi, 0)),
                  pl.BlockSpec((t, 1), lambda bi, pt: (bi, 0)),
                  pl.BlockSpec((N_KV, GROUP * t, 2 * KEY_BLK), lambda bi, pt: (0, 0, 0))]
        + _page_specs(n_pages, layer, LANES) + _page_specs(n_pages, layer, LANES),
        out_specs=pl.BlockSpec((1, t, D_ATT), lambda bi, pt: (bi, 0, 0)),
    )
    return pl.pallas_call(
        _dsa_sample_attn_kernel,
        grid_spec=grid_spec,
        out_shape=jax.ShapeDtypeStruct((b, t, D_ATT), F32),
        compiler_params=_cp(1),
        name="dsa_sample_attn",
    )(page_table, z, z, z, keys, thr, cut, bias_s, *([cache_k] * n_pages), *([cache_v] * n_pages))


def _merge_kernel(ya_ref, yb_ref, yc_ref, ga_ref, gb_ref, gc_ref, x_ref, g1_ref, sh2_ref, sc2_ref,
                  wpa_ref, wpb_ref, wpc_ref, wo_ref, lng_ref, lnb_ref, wrh_ref, wrl_ref, rb_ref,
                  x1_ref, h2_ref, gate_ref, *, alpha):
    gbk, rbk, d = x_ref.shape
    m = gbk * rbk

    def flat(ref):
        return ref[...].reshape(m, ref.shape[-1])

    merged = (_sigmoid(flat(ga_ref)) * _dot(flat(ya_ref).astype(BF16), wpa_ref[...])
              + _sigmoid(flat(gb_ref)) * _dot(flat(yb_ref).astype(BF16), wpb_ref[...])
              + _sigmoid(flat(gc_ref)) * _dot(flat(yc_ref).astype(BF16), wpc_ref[...]))
    mix = _dot(merged.astype(BF16), wo_ref[...]).reshape(gbk, rbk, d)
    x1 = _ln(alpha * x_ref[...] + (1.0 + g1_ref[...]) * mix, lng_ref[...], lnb_ref[...])
    x1_ref[...] = x1
    h2 = x1 * (1.0 + sc2_ref[...]) + sh2_ref[...]
    h2_ref[...] = h2.astype(h2_ref.dtype)

    h2f = h2.reshape(m, d)
    hi = h2f.astype(BF16)
    lo = (h2f - hi.astype(F32)).astype(BF16)
    logits = _dot_nt(wrh_ref[...], hi) + _dot_nt(wrl_ref[...], hi) + _dot_nt(wrh_ref[...], lo)
    scores = _sigmoid(logits)
    sel = scores + rb_ref[...]
    row = lax.broadcasted_iota(I32, (N_EXPERTS, m), 0).astype(F32)
    best = jnp.zeros((1, m), F32)
    best_score = None
    for g in range(N_GROUPS):
        r = [sel[g * EXPERTS_PER_GROUP + i:g * EXPERTS_PER_GROUP + i + 1] for i in range(EXPERTS_PER_GROUP)]
        top2 = None
        for i in range(EXPERTS_PER_GROUP):
            for j in range(i + 1, EXPERTS_PER_GROUP):
                pair = r[i] + r[j]
                top2 = pair if top2 is None else jnp.maximum(top2, pair)
        if best_score is None:
            best_score = top2
        else:
            better = top2 > best_score
            best = jnp.where(better, float(g), best)
            best_score = jnp.where(better, top2, best_score)
    lo_row = best * EXPERTS_PER_GROUP
    in_group = (row >= lo_row) & (row < lo_row + EXPERTS_PER_GROUP)
    masked = jnp.where(in_group, sel, -jnp.inf)
    m1 = jnp.max(masked, axis=0, keepdims=True)
    i1 = jnp.min(jnp.where(masked == m1, row, float(N_EXPERTS)), axis=0, keepdims=True)
    masked2 = jnp.where(row == i1, -jnp.inf, masked)
    m2 = jnp.max(masked2, axis=0, keepdims=True)
    i2 = jnp.min(jnp.where(masked2 == m2, row, float(N_EXPERTS)), axis=0, keepdims=True)
    w1 = jnp.sum(jnp.where(row == i1, scores, 0.0), axis=0, keepdims=True)
    w2 = jnp.sum(jnp.where(row == i2, scores, 0.0), axis=0, keepdims=True)
    tot = w1 + w2
    gate_ref[...] = jnp.where(row == i1, w1 / tot, 0.0) + jnp.where(row == i2, w2 / tot, 0.0)


def _merge(ya, yb, yc, z, x3, mod, w_pa, w_pb, w_pc, w_o, lng, lnb, wr_hi, wr_lo, rbias, gb, rb, act_dtype, alpha):
    g, r, d = x3.shape
    rt = r // rb

    def act(width, col):
        return pl.BlockSpec((gb, rb, width), lambda gi, ri: (gi, ri, col))

    def modspec(col):
        return pl.BlockSpec((gb, 1, d), lambda gi, ri: (gi, 0, col))

    def full(a):
        return pl.BlockSpec(a.shape, lambda gi, ri: (0,) * a.ndim)

    return pl.pallas_call(
        functools.partial(_merge_kernel, alpha=alpha),
        grid=(g // gb, rt),
        in_specs=[act(D_A, 0), act(D_ATT, 0), act(D_C, 0),
                  act(d, C_GA // d), act(d, C_GB // d), act(d, C_GC // d),
                  act(d, 0), modspec(2), modspec(3), modspec(4),
                  full(w_pa), full(w_pb), full(w_pc), full(w_o), full(lng), full(lnb),
                  full(wr_hi), full(wr_lo), full(rbias)],
        out_specs=[act(d, 0), act(d, 0),
                   pl.BlockSpec((N_EXPERTS, gb * rb), lambda gi, ri: (0, gi * rt + ri))],
        out_shape=[jax.ShapeDtypeStruct((g, r, d), F32),
                   jax.ShapeDtypeStruct((g, r, d), act_dtype),
                   jax.ShapeDtypeStruct((N_EXPERTS, g * r), F32)],
        compiler_params=_cp(2),
        name="merge",
    )(ya, yb, yc, z, z, z, x3, mod, mod, mod, w_pa, w_pb, w_pc, w_o, lng, lnb, wr_hi, wr_lo, rbias)


MOE_EXPERTS = 2


def _moe_kernel(h_ref, gate_ref, x1_ref, g2_ref, wg_ref, wu_ref, wd_ref, lng_ref, lnb_ref, o_ref, acc_scr, *, alpha):
    e = pl.program_id(2)
    gbk, rbk, d = h_ref.shape
    m = gbk * rbk

    @pl.when(e == 0)
    def _():
        acc_scr[...] = jnp.zeros(acc_scr.shape, F32)

    h = h_ref[...].reshape(m, d).astype(BF16)
    gate = gate_ref[...]
    lane = lax.broadcasted_iota(I32, gate.shape, 1)
    hes = []
    for j in range(MOE_EXPERTS):
        a = _dot(h, wg_ref[0, j])
        u = _dot(h, wu_ref[0, j])
        gcol = jnp.sum(jnp.where(lane == e * MOE_EXPERTS + j, gate, 0.0), axis=1, keepdims=True)
        hes.append(((a * _sigmoid(a)) * u * gcol).astype(BF16))
    wd = wd_ref[0].reshape(MOE_EXPERTS * D_EXPERT, d)
    acc_scr[...] += _dot(jnp.concatenate(hes, axis=1), wd)

    @pl.when(e == pl.num_programs(2) - 1)
    def _():
        y = acc_scr[...].reshape(gbk, rbk, d)
        o_ref[...] = _ln(alpha * x1_ref[...] + (1.0 + g2_ref[...]) * y, lng_ref[...], lnb_ref[...])


def _moe(h2, gate, x1, mod, w_g, w_u, w_d, lng, lnb, layer, gb, rb, alpha):
    g, r, d = x1.shape
    rt = r // rb
    n_e = w_g.shape[1]
    return pl.pallas_call(
        functools.partial(_moe_kernel, alpha=alpha),
        grid=(g // gb, rt, n_e // MOE_EXPERTS),
        in_specs=[pl.BlockSpec((gb, rb, d), lambda gi, ri, e: (gi, ri, 0)),
                  pl.BlockSpec((gb * rb, n_e), lambda gi, ri, e: (gi * rt + ri, 0)),
                  pl.BlockSpec((gb, rb, d), lambda gi, ri, e: (gi, ri, 0)),
                  pl.BlockSpec((gb, 1, d), lambda gi, ri, e: (gi, 0, 5)),
                  pl.BlockSpec((1, MOE_EXPERTS, d, D_EXPERT), lambda gi, ri, e: (layer, e, 0, 0)),
                  pl.BlockSpec((1, MOE_EXPERTS, d, D_EXPERT), lambda gi, ri, e: (layer, e, 0, 0)),
                  pl.BlockSpec((1, MOE_EXPERTS, D_EXPERT, d), lambda gi, ri, e: (layer, e, 0, 0)),
                  pl.BlockSpec((1, d), lambda gi, ri, e: (0, 0)),
                  pl.BlockSpec((1, d), lambda gi, ri, e: (0, 0))],
        out_specs=pl.BlockSpec((gb, rb, d), lambda gi, ri, e: (gi, ri, 0)),
        out_shape=jax.ShapeDtypeStruct((g, r, d), F32),
        scratch_shapes=[pltpu.VMEM((gb * rb, d), F32)],
        compiler_params=_cp(3),
        name="moe",
    )(h2, gate, x1, mod, w_g, w_u, w_d, lng, lnb)


def _pad_cols(n, dtype, like):
    return jnp.zeros(like.shape[:-1] + (n,), dtype)


def _relayout_in(w):
    def sl(o, n):
        return w[..., o:o + n]
    ki = sl(_O_KI, D_IDX)
    parts = [sl(_O_AGLU, 2 * D_A), sl(_O_UV, 2 * D_C), sl(_O_G, 3 * D_MODEL), sl(_O_Q, D_ATT),
             sl(_O_QI, N_IDX_HEADS * D_IDX), sl(_O_K, N_KV * HEAD_DIM), sl(_O_V, N_KV * HEAD_DIM),
             ki, ki, ki, ki, sl(_O_WI, N_IDX_HEADS), _pad_cols(D_ZP - C_WI - N_IDX_HEADS, w.dtype, w)]
    return jnp.concatenate(parts, axis=-1)


def kernel(x_prompt, x_sample, cache_k, cache_v, cache_kidx, state_conv, page_table, c_prompt, c_sample,
           w_ada, b_ada, w_in, b_in, conv_k, conv_b, lnA_g, lnA_b, lnC_g, lnC_b, w_s, b_s,
           w_pa, w_pb, w_pc, w_o, ln1_g, ln1_b, ln2_g, ln2_b, rel_table, w_router, router_bias,
           w_e_gate, w_e_up, w_e_down):
    depth = w_in.shape[0]
    alpha = (2 * depth) ** 0.25
    bp, tp, d = x_prompt.shape
    bs, ts, _ = x_sample.shape
    n_pages = page_table.shape[1]
    past = n_pages * PAGE_SIZE
    topk_s = min(TOPK_MAX, (past + ts) // 4)
    n_pool = cache_k.shape[1]

    w_in_p = _relayout_in(w_in).astype(BF16)
    b_in_p = _relayout_in(b_in).reshape(depth, 1, D_ZP)
    w_pa_b, w_pb_b, w_pc_b, w_o_b = (w.astype(BF16) for w in (w_pa, w_pb, w_pc, w_o))
    w_g_b, w_u_b, w_d_b = (w.astype(BF16) for w in (w_e_gate, w_e_up, w_e_down))
    wr_t = w_router.T
    wr_hi = wr_t.astype(BF16)
    wr_lo = (wr_t - wr_hi.astype(F32)).astype(BF16)
    rbias = router_bias.reshape(N_EXPERTS, 1)
    grp_w = D_C // N_CGROUPS
    bs_tab_p = jnp.repeat(jnp.swapaxes(b_s[:, :, :CHUNK], 1, 2), grp_w, axis=2)
    bs_tab_s = bs_tab_p[:, :ts]
    wl_tab = jnp.repeat(jnp.transpose(w_s[:, :, :ts, :ts], (0, 3, 2, 1)), grp_w, axis=3)
    cache_k2 = jnp.transpose(cache_k, (0, 1, 3, 4, 2)).reshape(depth, n_pool, N_KV * HEAD_DIM, PAGE_SIZE)
    cache_v2 = jnp.transpose(cache_v, (0, 1, 3, 4, 2)).reshape(depth, n_pool, N_KV * HEAD_DIM, PAGE_SIZE)
    cache_ki2 = jnp.transpose(cache_kidx, (0, 1, 3, 2))

    def row(a, l):
        return a[l].reshape(1, -1)

    bias_p = _bias_near(rel_table)
    bias_s = bias_p.reshape(N_KV, GROUP, KEY_BLK, 2 * KEY_BLK)[:, :, :ts].reshape(N_KV, GROUP * ts, 2 * KEY_BLK)

    n_c = bp + bs
    c_all = jnp.concatenate([c_prompt, c_sample, jnp.zeros((-n_c % 8, d), F32)], axis=0)
    mod_all = _ada(c_all, w_ada, b_ada)

    yp, ys = x_prompt, x_sample
    outs = {k: [] for k in ("kp", "vp", "kip", "cp", "ks", "vs", "kis", "cs", "vcs")}
    rb_p = 512
    for l in range(depth):
        mod_p = mod_all[l, :bp].reshape(bp, 1, 6 * d)
        mod_s = mod_all[l, bp:n_c].reshape(bs, 1, 6 * d)
        lnA = (row(lnA_g, l), row(lnA_b, l))
        lnC = (row(lnC_g, l), row(lnC_b, l))
        ln1 = (row(ln1_g, l), row(ln1_b, l))
        ln2 = (row(ln2_g, l), row(ln2_b, l))

        z = _inproj(yp, mod_p, w_in_p[l], b_in_p[l], 1, rb_p, 3328)
        ya, conv_state = _conv_prompt(z, conv_k[l], row(conv_b, l), *lnA)
        yb = _dsa_prompt(z, bias_p)
        yc = _sgu_prompt(z, *lnC, w_s[l], bs_tab_p[l])
        x1, h2, gate_t = _merge(ya, yb, yc, z, yp, mod_p, w_pa_b[l], w_pb_b[l], w_pc_b[l], w_o_b[l], *ln1,
                                wr_hi, wr_lo, rbias, 1, rb_p, BF16, alpha)
        yp = _moe(h2, gate_t.T, x1, mod_p, w_g_b, w_u_b, w_d_b, *ln2, l, 1, 1024, alpha)
        outs["kp"].append(z[:, :, C_K:C_K + N_KV * HEAD_DIM].reshape(bp, tp, N_KV, HEAD_DIM))
        outs["vp"].append(z[:, :, C_VV:C_VV + N_KV * HEAD_DIM].reshape(bp, tp, N_KV, HEAD_DIM))
        outs["kip"].append(z[:, :, C_KI4:C_KI4 + D_IDX])
        outs["cp"].append(conv_state)

        zs = _inproj(ys, mod_s, w_in_p[l], b_in_p[l], bs, ts, 1664)
        ya, conv_state = _conv_sample(zs, state_conv[l], conv_k[l], row(conv_b, l), *lnA)
        keys = _dsa_sample_scores(zs, cache_ki2, page_table, l)
        thr, cut = _dsa_sample_topk(keys, topk_s)
        yb = _dsa_sample_attn(zs, keys, thr, cut, cache_k2, cache_v2, page_table, l, bias_s)
        yc, v_chunk = _sgu_sample(zs, *lnC, wl_tab[l], bs_tab_s[l])
        x1, h2, gate_t = _merge(ya, yb, yc, zs, ys, mod_s, w_pa_b[l], w_pb_b[l], w_pc_b[l], w_o_b[l], *ln1,
                                wr_hi, wr_lo, rbias, bs, ts, F32, alpha)
        ys = _moe(h2, gate_t.T, x1, mod_s, w_g_b, w_u_b, w_d_b, *ln2, l, bs, ts, alpha)
        outs["ks"].append(zs[:, :, C_K:C_K + N_KV * HEAD_DIM].reshape(bs, ts, N_KV, HEAD_DIM))
        outs["vs"].append(zs[:, :, C_VV:C_VV + N_KV * HEAD_DIM].reshape(bs, ts, N_KV, HEAD_DIM))
        outs["kis"].append(zs[:, :, C_KI4:C_KI4 + D_IDX])
        outs["cs"].append(conv_state)
        outs["vcs"].append(v_chunk)

    st = {k: jnp.stack(v) for k, v in outs.items()}
    return (yp, ys, st["kp"], st["vp"], st["kip"], st["cp"],
            st["ks"], st["vs"], st["kis"], st["cs"], st["vcs"])
```

```python
import functools
import math

import jax
import jax.numpy as jnp
from jax import lax
from jax.experimental import pallas as pl
from jax.experimental.pallas import tpu as pltpu

F32 = jnp.float32
BF16 = jnp.bfloat16
I32 = jnp.int32

D_MODEL = 1024
D_A = 512
CONV_W = 31
N_HEADS = 8
N_KV = 2
GROUP = N_HEADS // N_KV
HEAD_DIM = 64
D_ATT = N_HEADS * HEAD_DIM
N_IDX_HEADS = 4
D_IDX = 64
TOPK_MAX = 256
N_BUCKETS = 32
MAX_DIST = 128
D_C = 512
N_CGROUPS = 8
CHUNK = 128
N_EXPERTS = 16
N_GROUPS = 4
EXPERTS_PER_GROUP = N_EXPERTS // N_GROUPS
D_EXPERT = 512
PAGE_SIZE = 128
LN_EPS = 1e-5

LANES = 128
KEY_BLK = 128
FAR_BLKS = 4
Q_PAIR = 4
NEG_BIG = -1e30
LOG2E = math.log2(math.e)
INT_MIN = -2 ** 31
VMEM_LIMIT = 56 * 1024 * 1024

C_A, C_AG, C_U, C_V = 0, 512, 1024, 1536
C_GA, C_GB, C_GC = 2048, 3072, 4096
C_Q, C_QI, C_K, C_VV, C_KI4, C_WI = 5120, 5632, 5888, 6016, 6144, 6400
D_ZP = 6656

_O_AGLU, _O_Q, _O_K, _O_V, _O_QI, _O_KI, _O_WI, _O_UV, _O_G = 0, 1024, 1536, 1664, 1792, 2048, 2112, 2116, 3140
_D_IN = 6212


def _cp(n_axes):
    return pltpu.CompilerParams(dimension_semantics=("arbitrary",) * n_axes, vmem_limit_bytes=VMEM_LIMIT)


def _sigmoid(x):
    return jax.nn.sigmoid(x)


def _ln(x, g, b):
    mu = jnp.mean(x, axis=-1, keepdims=True)
    xc = x - mu
    var = jnp.mean(xc * xc, axis=-1, keepdims=True)
    return xc * lax.rsqrt(var + LN_EPS) * g + b


def _dot(a, b):
    return jnp.dot(a, b, preferred_element_type=F32)


def _dot_nt(a, b):
    return lax.dot_general(a, b, (((1,), (1,)), ((), ())), preferred_element_type=F32)


def _ada_kernel(c_ref, w_ref, b_ref, o_ref):
    c = c_ref[...]
    s = (c * _sigmoid(c)).astype(BF16)
    o_ref[0] = _dot(s, w_ref[0].astype(BF16)) + b_ref[0]


def _ada(c_all, w_ada, b_ada):
    depth, d, n = w_ada.shape
    m = c_all.shape[0]
    tn = 1536
    return pl.pallas_call(
        _ada_kernel,
        grid=(depth, n // tn),
        in_specs=[pl.BlockSpec((m, d), lambda l, j: (0, 0)),
                  pl.BlockSpec((1, d, tn), lambda l, j: (l, 0, j)),
                  pl.BlockSpec((1, 1, tn), lambda l, j: (l, 0, j))],
        out_specs=pl.BlockSpec((1, m, tn), lambda l, j: (l, 0, j)),
        out_shape=jax.ShapeDtypeStruct((depth, m, n), F32),
        compiler_params=_cp(2),
        name="ada",
    )(c_all, w_ada, b_ada.reshape(depth, 1, n))


def _inproj_kernel(x_ref, sh_ref, sc_ref, w_ref, b_ref, z_ref):
    gb, rb, d = x_ref.shape
    h = x_ref[...] * (1.0 + sc_ref[...]) + sh_ref[...]
    z = _dot(h.reshape(gb * rb, d).astype(BF16), w_ref[...]) + b_ref[...]
    z_ref[...] = z.reshape(gb, rb, z.shape[-1])


def _inproj(x3, mod, w, b, gb, rb, tn):
    g, r, d = x3.shape
    n = w.shape[1]
    return pl.pallas_call(
        _inproj_kernel,
        grid=(n // tn, g // gb, r // rb),
        in_specs=[pl.BlockSpec((gb, rb, d), lambda j, gi, ri: (gi, ri, 0)),
                  pl.BlockSpec((gb, 1, d), lambda j, gi, ri: (gi, 0, 0)),
                  pl.BlockSpec((gb, 1, d), lambda j, gi, ri: (gi, 0, 1)),
                  pl.BlockSpec((d, tn), lambda j, gi, ri: (0, j)),
                  pl.BlockSpec((1, tn), lambda j, gi, ri: (0, j))],
        out_specs=pl.BlockSpec((gb, rb, tn), lambda j, gi, ri: (gi, ri, j)),
        out_shape=jax.ShapeDtypeStruct((g, r, n), F32),
        compiler_params=_cp(3),
        name="inproj",
    )(x3, mod, mod, w, b)


CONV_HALO = 32
CONV_ROWS = 32


SUBLANES = 8


def _conv_prompt_kernel(a_ref, g_ref, ah_ref, gh_ref, ck_ref, cb_ref, lg_ref, lb_ref, y_ref, st_ref,
                        ext_scr, sh_scr):
    t = pl.program_id(1)
    tt = a_ref.shape[1]
    a = a_ref[0] * _sigmoid(g_ref[0])
    halo = ah_ref[0] * _sigmoid(gh_ref[0])
    ext_scr[0:CONV_HALO, :] = jnp.where(t > 0, halo, 0.0)
    ext_scr[CONV_HALO:, :] = a
    off = CONV_HALO - (CONV_W - 1)
    ext_rows = tt + CONV_HALO
    for ph in range(SUBLANES):
        n_rows = ext_rows if ph == 0 else ext_rows - SUBLANES
        sh_scr[ph, 0:n_rows, :] = ext_scr[ph:ph + n_rows, :]
    for c in range(tt // CONV_ROWS):
        r0 = c * CONV_ROWS
        acc = jnp.zeros((CONV_ROWS, D_A), F32) + cb_ref[...]
        for j in range(CONV_W):
            ph, base = (off + j) % SUBLANES, (off + j) // SUBLANES * SUBLANES
            acc = acc + ck_ref[j:j + 1, :] * sh_scr[ph, r0 + base:r0 + base + CONV_ROWS, :]
        y = _ln(acc, lg_ref[...], lb_ref[...])
        y_ref[0, r0:r0 + CONV_ROWS, :] = (y * _sigmoid(y)).astype(y_ref.dtype)

    @pl.when(t == pl.num_programs(1) - 1)
    def _():
        st_ref[0] = ext_scr[tt + off:tt + CONV_HALO, :]


def _conv_prompt(z, ck, cb, lg, lb, tt=256):
    b, t, _ = z.shape
    hb = tt // CONV_HALO
    return pl.pallas_call(
        _conv_prompt_kernel,
        grid=(b, t // tt),
        in_specs=[pl.BlockSpec((1, tt, D_A), lambda bi, ti: (bi, ti, C_A // D_A)),
                  pl.BlockSpec((1, tt, D_A), lambda bi, ti: (bi, ti, C_AG // D_A)),
                  pl.BlockSpec((1, CONV_HALO, D_A), lambda bi, ti: (bi, jnp.maximum(ti * hb - 1, 0), C_A // D_A)),
                  pl.BlockSpec((1, CONV_HALO, D_A), lambda bi, ti: (bi, jnp.maximum(ti * hb - 1, 0), C_AG // D_A)),
                  pl.BlockSpec((CONV_W, D_A), lambda bi, ti: (0, 0)),
                  pl.BlockSpec((1, D_A), lambda bi, ti: (0, 0)),
                  pl.BlockSpec((1, D_A), lambda bi, ti: (0, 0)),
                  pl.BlockSpec((1, D_A), lambda bi, ti: (0, 0))],
        out_specs=[pl.BlockSpec((1, tt, D_A), lambda bi, ti: (bi, ti, 0)),
                   pl.BlockSpec((1, CONV_W - 1, D_A), lambda bi, ti: (bi, 0, 0))],
        out_shape=[jax.ShapeDtypeStruct((b, t, D_A), BF16),
                   jax.ShapeDtypeStruct((b, CONV_W - 1, D_A), F32)],
        scratch_shapes=[pltpu.VMEM((tt + CONV_HALO, D_A), F32),
                        pltpu.VMEM((SUBLANES, tt + CONV_HALO, D_A), F32)],
        compiler_params=_cp(2),
        name="conv_prompt",
    )(z, z, z, z, ck, cb, lg, lb)


def _conv_sample_kernel(a_ref, g_ref, st_ref, ck_ref, cb_ref, lg_ref, lb_ref, y_ref, ns_ref, ext_scr):
    bs, t, _ = a_ref.shape
    hist = CONV_W - 1
    ext_scr[:, 0:hist, :] = st_ref[...]
    ext_scr[:, hist:hist + t, :] = a_ref[...] * _sigmoid(g_ref[...])
    ns_ref[...] = ext_scr[:, t:t + hist, :]

    def body(b, carry):
        acc = jnp.zeros((t, D_A), F32) + cb_ref[...]
        for j in range(CONV_W):
            acc = acc + ck_ref[j:j + 1, :] * ext_scr[b, j:j + t, :]
        y = _ln(acc, lg_ref[...], lb_ref[...])
        y_ref[b] = y * _sigmoid(y)
        return carry

    lax.fori_loop(0, bs, body, 0)


def _conv_sample(z, state, ck, cb, lg, lb, bs=32):
    b, t, _ = z.shape
    hist = CONV_W - 1
    bs = min(bs, b)
    return pl.pallas_call(
        _conv_sample_kernel,
        grid=(b // bs,),
        in_specs=[pl.BlockSpec((bs, t, D_A), lambda bi: (bi, 0, C_A // D_A)),
                  pl.BlockSpec((bs, t, D_A), lambda bi: (bi, 0, C_AG // D_A)),
                  pl.BlockSpec((bs, hist, D_A), lambda bi: (bi, 0, 0)),
                  pl.BlockSpec((CONV_W, D_A), lambda bi: (0, 0)),
                  pl.BlockSpec((1, D_A), lambda bi: (0, 0)),
                  pl.BlockSpec((1, D_A), lambda bi: (0, 0)),
                  pl.BlockSpec((1, D_A), lambda bi: (0, 0))],
        out_specs=[pl.BlockSpec((bs, t, D_A), lambda bi: (bi, 0, 0)),
                   pl.BlockSpec((bs, hist, D_A), lambda bi: (bi, 0, 0))],
        out_shape=[jax.ShapeDtypeStruct((b, t, D_A), F32),
                   jax.ShapeDtypeStruct((b, hist, D_A), F32)],
        scratch_shapes=[pltpu.VMEM((bs, hist + t + 2, D_A), F32)],
        compiler_params=_cp(1),
        name="conv_sample",
    )(z, z, state, ck, cb, lg, lb)


def _sgu_prompt_kernel(u_ref, v_ref, lg_ref, lb_ref, ws_ref, bs_ref, y_ref):
    tt = u_ref.shape[1]
    vv = _ln(v_ref[0], lg_ref[...], lb_ref[...])
    row = lax.broadcasted_iota(I32, (CHUNK, CHUNK), 0)
    col = lax.broadcasted_iota(I32, (CHUNK, CHUNK), 1)
    lane_grp = lax.broadcasted_iota(I32, (CHUNK, D_C), 1) // (D_C // N_CGROUPS)
    ws = [jnp.where(col <= row, ws_ref[g], 0.0).astype(BF16) for g in range(N_CGROUPS)]
    for c in range(tt // CHUNK):
        vc = vv[c * CHUNK:(c + 1) * CHUNK].astype(BF16)
        mixed = bs_ref[...]
        for g in range(N_CGROUPS):
            mixed = mixed + _dot(ws[g], jnp.where(lane_grp == g, vc, jnp.zeros_like(vc)))
        y_ref[0, c * CHUNK:(c + 1) * CHUNK, :] = (u_ref[0, c * CHUNK:(c + 1) * CHUNK, :] * mixed).astype(y_ref.dtype)


def _sgu_prompt(z, lg, lb, w_s, bs_tab, tt=512):
    b, t, _ = z.shape
    return pl.pallas_call(
        _sgu_prompt_kernel,
        grid=(b, t // tt),
        in_specs=[pl.BlockSpec((1, tt, D_C), lambda bi, ti: (bi, ti, C_U // D_C)),
                  pl.BlockSpec((1, tt, D_C), lambda bi, ti: (bi, ti, C_V // D_C)),
                  pl.BlockSpec((1, D_C), lambda bi, ti: (0, 0)),
                  pl.BlockSpec((1, D_C), lambda bi, ti: (0, 0)),
                  pl.BlockSpec((N_CGROUPS, CHUNK, CHUNK), lambda bi, ti: (0, 0, 0)),
                  pl.BlockSpec((CHUNK, D_C), lambda bi, ti: (0, 0))],
        out_specs=pl.BlockSpec((1, tt, D_C), lambda bi, ti: (bi, ti, 0)),
        out_shape=jax.ShapeDtypeStruct((b, t, D_C), BF16),
        compiler_params=_cp(2),
        name="sgu_prompt",
    )(z, z, lg, lb, w_s, bs_tab)


def _sgu_sample_kernel(u_ref, v_ref, lg_ref, lb_ref, wl_ref, bs_ref, y_ref, vc_ref):
    bs, t, _ = u_ref.shape
    vv = _ln(v_ref[...], lg_ref[...], lb_ref[...])
    vc_ref[...] = vv
    tpos = lax.broadcasted_iota(I32, (t, D_C), 0)
    mixed = jnp.zeros((bs, t, D_C), F32) + bs_ref[...]
    for s in range(t):
        w = jnp.where(tpos >= s, wl_ref[s], 0.0)
        mixed = mixed + w * vv[:, s:s + 1, :]
    y_ref[...] = u_ref[...] * mixed


def _sgu_sample(z, lg, lb, wl_tab, bs_tab):
    b, t, _ = z.shape
    return pl.pallas_call(
        _sgu_sample_kernel,
        grid=(1,),
        in_specs=[pl.BlockSpec((b, t, D_C), lambda i: (0, 0, C_U // D_C)),
                  pl.BlockSpec((b, t, D_C), lambda i: (0, 0, C_V // D_C)),
                  pl.BlockSpec((1, D_C), lambda i: (0, 0)),
                  pl.BlockSpec((1, D_C), lambda i: (0, 0)),
                  pl.BlockSpec((t, t, D_C), lambda i: (0, 0, 0)),
                  pl.BlockSpec((t, D_C), lambda i: (0, 0))],
        out_specs=[pl.BlockSpec((b, t, D_C), lambda i: (0, 0, 0)),
                   pl.BlockSpec((b, t, D_C), lambda i: (0, 0, 0))],
        out_shape=[jax.ShapeDtypeStruct((b, t, D_C), F32),
                   jax.ShapeDtypeStruct((b, t, D_C), F32)],
        compiler_params=_cp(1),
        name="sgu_sample",
    )(z, z, lg, lb, wl_tab, bs_tab)


def _bias_kernel(tab_ref, o_ref):
    rows, cols = KEY_BLK, 2 * KEY_BLK
    i = lax.broadcasted_iota(I32, (rows, cols), 0)
    j = lax.broadcasted_iota(I32, (rows, cols), 1)
    n = jnp.maximum(KEY_BLK + i - j, 0)
    max_exact = N_BUCKETS // 2
    nf = jnp.maximum(n, 1).astype(F32)
    large = max_exact + (jnp.log(nf / max_exact) / math.log(MAX_DIST / max_exact) * (N_BUCKETS - max_exact)).astype(I32)
    large = jnp.minimum(large, N_BUCKETS - 1)
    bucket = jnp.where(n < max_exact, n, large)
    for h in range(N_HEADS):
        acc = jnp.zeros((rows, cols), F32)
        for bkt in range(N_BUCKETS):
            acc = jnp.where(bucket == bkt, tab_ref[bkt, h], acc)
        o_ref[h // GROUP, (h % GROUP) * rows:(h % GROUP + 1) * rows, :] = (acc - tab_ref[N_BUCKETS - 1, h]) * LOG2E


def _bias_near(rel_table):
    return pl.pallas_call(
        _bias_kernel,
        grid=(1,),
        in_specs=[pl.BlockSpec(memory_space=pltpu.SMEM)],
        out_specs=pl.BlockSpec((N_KV, GROUP * KEY_BLK, 2 * KEY_BLK), lambda i: (0, 0, 0)),
        out_shape=jax.ShapeDtypeStruct((N_KV, GROUP * KEY_BLK, 2 * KEY_BLK), F32),
        compiler_params=_cp(1),
        name="bias_near",
    )(rel_table)


def _score_keys(s):
    s = jnp.where(s == 0.0, 0.0, s)
    bits = lax.bitcast_convert_type(s, I32)
    return bits ^ (jnp.right_shift(bits, 31) & 0x7FFFFFFF)


def _lane_top2(s, m1, m2):
    for blk in [s[:, c * LANES:(c + 1) * LANES] for c in range(s.shape[1] // LANES)]:
        m2 = jnp.maximum(m2, jnp.minimum(m1, blk))
        m1 = jnp.maximum(m1, blk)
    return m1, m2


def _topk_threshold(load_blk, n_iter, unroll, rows, k, idx_bits, cut_scr, bounds=None, parts=1):
    lane = lax.broadcasted_iota(I32, (rows, KEY_BLK), 1)
    groups = range(parts)
    if bounds is None:
        first_bit = 0
        thr0 = [jnp.full((rows, 1), INT_MIN, I32) for _ in groups]
    else:
        shared = [jnp.min(lax.clz(lo ^ hi)) for lo, hi in bounds]
        first_bit = jnp.minimum(functools.reduce(jnp.minimum, shared), 31)
        keep = ~(jnp.left_shift(jnp.int32(2), 31 - first_bit) - 1)
        thr0 = [((hi ^ INT_MIN) & keep) ^ INT_MIN for _, hi in bounds]

    def count(preds):
        accs = []
        for p in groups:
            def body(jj, acc, p=p):
                for u in range(unroll):
                    j = jj * unroll + u
                    acc = jnp.where(preds[p](load_blk(p, j), j), acc + 1.0, acc)
                return acc
            accs.append(lax.fori_loop(0, n_iter, body, jnp.zeros((rows, KEY_BLK), F32)))
        return [jnp.sum(a, axis=1, keepdims=True) for a in accs]

    def ge(xs):
        return [lambda kb, j, x=x: kb >= x for x in xs]

    kf = float(k)

    def bit_body(i, thr):
        cand = [t ^ jnp.left_shift(jnp.int32(1), 31 - i) for t in thr]
        cnt = count(ge(cand))
        return [jnp.where(cnt[p] >= kf, cand[p], thr[p]) for p in groups]

    thr = lax.fori_loop(first_bit, 32, bit_body, thr0)
    cnt_gt = count([lambda kb, j, t=t: kb > t for t in thr])
    need = [kf - c for c in cnt_gt]
    if cut_scr is None:
        return thr, need
    assert parts == 1
    thr, need = thr[0], need[0]
    cnt_ge = count(ge([thr]))[0]
    cut_scr[...] = jnp.full((rows, 1), 2 ** idx_bits, I32)

    @pl.when(jnp.max(cnt_ge) > kf)
    def _():
        def idx_body(i, r):
            cand = r + jnp.left_shift(jnp.int32(1), idx_bits - 1 - i)
            cnt = count([lambda kb, j: (kb == thr) & (j * KEY_BLK + lane < cand)])[0]
            return jnp.where(cnt < need, cand, r)
        r = lax.fori_loop(0, idx_bits, idx_body, jnp.zeros((rows, 1), I32))
        cut_scr[...] = r + 1

    return thr, cut_scr[...]


def _select_mask(keys, thr, cut, gidx):
    return (keys > thr) | ((keys == thr) & (gidx < cut))


def _stack_heads(q, n, rows):
    half = lax.broadcasted_iota(I32, (rows, LANES), 1) // HEAD_DIM
    tiles = []
    for g in range(GROUP):
        h = n * GROUP + g
        col = q[:, (h // 2) * LANES:(h // 2 + 1) * LANES]
        if h % 2 != n:
            col = pltpu.roll(col, HEAD_DIM, 1)
        tiles.append(jnp.where(half == n, col, 0.0))
    return jnp.concatenate(tiles, axis=0)


def _unstack_heads(o_groups, rows):
    half = lax.broadcasted_iota(I32, (rows, LANES), 1) // HEAD_DIM
    cols = []
    for c in range(N_HEADS // 2):
        parts = []
        for h in (2 * c, 2 * c + 1):
            n, g = h // GROUP, h % GROUP
            tile = o_groups[n][g * rows:(g + 1) * rows, :]
            if h % 2 != n:
                tile = pltpu.roll(tile, HEAD_DIM, 1)
            parts.append(tile)
        cols.append(jnp.where(half == 0, parts[0], parts[1]))
    return jnp.concatenate(cols, axis=1)


def _dsa_prompt_kernel(q_ref, qi_ref, wi_ref, ki_ref, k_ref, v_ref, bias_ref, o_ref,
                       key_scr, qs_scr, mpart_scr, mfull_scr, lpart_scr, acc_scr, p_scr,
                       lgfar_scr, lgnear_scr, *, topk):
    pair = pl.program_id(1)
    rows = KEY_BLK
    chunk = FAR_BLKS * KEY_BLK
    row_i = lax.broadcasted_iota(I32, (rows, 1), 0)
    lane_c = lax.broadcasted_iota(I32, (rows, chunk), 1)
    n_chunks = (pair * Q_PAIR) // FAR_BLKS + 1
    assert FAR_BLKS % Q_PAIR == 0

    def scores(s):
        sub = slice(s * rows, (s + 1) * rows)
        tpos = (pair * Q_PAIR + s) * rows + row_i
        qi = qi_ref[0, sub, :]
        lane_head = lax.broadcasted_iota(I32, qi.shape, 1) // D_IDX
        qim = [jnp.where(lane_head == h, qi, 0.0).astype(BF16) for h in range(N_IDX_HEADS)]
        wi = wi_ref[0, sub, :]

        def score_chunk(c, top2):
            kc = ki_ref[0, pl.ds(pl.multiple_of(c * chunk, chunk), chunk), :].astype(BF16)
            sc = jnp.zeros((rows, chunk), F32)
            for h in range(N_IDX_HEADS):
                sc = sc + wi[:, h:h + 1] * jnp.maximum(_dot_nt(qim[h], kc), 0.0)
            sc = jnp.where(c * chunk + lane_c <= tpos, sc, -jnp.inf)
            keys = _score_keys(sc)
            for j in range(FAR_BLKS):
                key_scr[s, c * FAR_BLKS + j] = keys[:, j * KEY_BLK:(j + 1) * KEY_BLK]
            return _lane_top2(sc, *top2)

        def score_body(i, top2):
            return score_chunk(2 * i + 1, score_chunk(2 * i, top2))

        ninf = jnp.full((rows, LANES), -jnp.inf, F32)
        top2 = lax.fori_loop(0, n_chunks // 2, score_body, (ninf, ninf))
        m1, m2 = lax.fori_loop(0, n_chunks % 2, lambda i, t: score_chunk(n_chunks - 1, t), top2)
        return (_score_keys(jnp.min(m2, axis=1, keepdims=True)), _score_keys(jnp.max(m1, axis=1, keepdims=True)))

    assert topk <= 2 * LANES
    bounds = [scores(s) for s in range(Q_PAIR)]
    thr_all, need_all = _topk_threshold(lambda s, j: key_scr[s, j], n_chunks, FAR_BLKS, rows, topk, 0, None,
                                        bounds, parts=Q_PAIR)

    for s in range(Q_PAIR):
        sub = slice(s * rows, (s + 1) * rows)
        _dsa_prompt_attend(pair * Q_PAIR + s, thr_all[s], need_all[s], q_ref.at[0, sub, :], k_ref, v_ref,
                           bias_ref, o_ref.at[0, sub, :], key_scr.at[s], qs_scr, mpart_scr, mfull_scr, lpart_scr,
                           acc_scr, p_scr, lgfar_scr, lgnear_scr)


def _dsa_prompt_attend(qb, thr, need, q_ref, k_ref, v_ref, bias_ref, o_ref, key_scr, qs_scr, mpart_scr, mfull_scr,
                       lpart_scr, acc_scr, p_scr, lgfar_scr, lgnear_scr):
    rows = KEY_BLK
    chunk = FAR_BLKS * KEY_BLK
    tpos = qb * rows + lax.broadcasted_iota(I32, (rows, 1), 0)

    ri = lax.broadcasted_iota(I32, (KEY_BLK, KEY_BLK), 0)
    ci = lax.broadcasted_iota(I32, (KEY_BLK, KEY_BLK), 1)
    prefix_mat = jnp.where(ri <= ci, 1.0, 0.0).astype(BF16)
    ones_mat = jnp.ones((KEY_BLK, KEY_BLK), BF16)

    def select_block(kb, run):
        eq = kb == thr
        e = jnp.where(eq, 1.0, 0.0).astype(BF16)
        sel = (kb > thr) | (eq & (run + _dot(e, prefix_mat) <= need))
        return sel, run + _dot(e, ones_mat)

    q = q_ref[...] * (HEAD_DIM ** -0.5 * LOG2E)
    for n in range(N_KV):
        qs_scr[n] = _stack_heads(q, n, rows).astype(BF16)
    mpart_scr[...] = jnp.full(mpart_scr.shape, NEG_BIG, F32)
    lpart_scr[...] = jnp.zeros(lpart_scr.shape, F32)
    acc_scr[...] = jnp.zeros(acc_scr.shape, F32)

    def lane_blocks(t):
        return [t[:, c * LANES:(c + 1) * LANES] for c in range(t.shape[1] // LANES)]

    def fold(vals, op):
        while len(vals) > 1:
            vals = [op(vals[i], vals[i + 1]) for i in range(0, len(vals), 2)]
        return vals[0]

    def sweep_max(kc, neg, with_bias, lg_ref):
        for n in range(N_KV):
            lg = _dot_nt(qs_scr[n], kc)
            for g in range(GROUP):
                sl = slice(g * rows, (g + 1) * rows)
                t = lg[sl] + neg
                if with_bias:
                    t = t + bias_ref[n, sl, :]
                lg_ref[n, sl, :] = t
                mpart_scr[n, sl, :] = jnp.maximum(mpart_scr[n, sl, :], fold(lane_blocks(t), jnp.maximum))

    def sweep_acc(vc, lg_ref):
        width = vc.shape[0]
        for n in range(N_KV):
            for g in range(GROUP):
                sl = slice(g * rows, (g + 1) * rows)
                mf = mfull_scr[n, sl, :]
                ps = [jnp.exp2(tb - mf) for tb in lane_blocks(lg_ref[n, sl, :])]
                lpart_scr[n, sl, :] = lpart_scr[n, sl, :] + fold(ps, jnp.add)
                p_scr[sl, 0:width] = jnp.concatenate(ps, axis=1).astype(BF16)
            acc_scr[n] = acc_scr[n] + _dot(p_scr[:, 0:width], vc)

    n_far = (qb + 2) // FAR_BLKS

    def far_max(c, run):
        start = pl.multiple_of(c * chunk, chunk)
        kc = k_ref[0, pl.ds(start, chunk), :].astype(BF16)
        negs = []
        for j in range(FAR_BLKS):
            blk = c * FAR_BLKS + j
            is_far = blk < qb - 1
            sel, run_next = select_block(key_scr[blk], run)
            run = jnp.where(is_far, run_next, run)
            negs.append(jnp.where(sel & is_far, 0.0, NEG_BIG))
        sweep_max(kc, jnp.concatenate(negs, axis=1), False, lgfar_scr.at[c])
        return run

    run = lax.fori_loop(0, n_far // 2, lambda i, r: far_max(2 * i + 1, far_max(2 * i, r)),
                        jnp.zeros((rows, KEY_BLK), F32))
    run = lax.fori_loop(0, n_far % 2, lambda i, r: far_max(n_far - 1, r), run)

    pb = jnp.maximum(qb - 1, 0)
    p0 = pl.multiple_of(pb * rows, rows)
    q0 = pl.multiple_of(qb * rows, rows)
    lane_b = lax.broadcasted_iota(I32, (rows, KEY_BLK), 1)
    sel_prev, run_next = select_block(key_scr[pb], run)
    run = jnp.where(qb > 0, run_next, run)
    sel_prev = sel_prev & (qb > 0)
    sel_diag, _ = select_block(key_scr[qb], run)
    sel_diag = sel_diag & (qb * rows + lane_b <= tpos)
    neg_near = jnp.concatenate([jnp.where(sel_prev, 0.0, NEG_BIG), jnp.where(sel_diag, 0.0, NEG_BIG)], axis=1)

    def near_kv(ref):
        return jnp.concatenate([ref[0, pl.ds(p0, rows), :], ref[0, pl.ds(q0, rows), :]], axis=0).astype(BF16)

    sweep_max(near_kv(k_ref), neg_near, True, lgnear_scr)
    for n in range(N_KV):
        mfull_scr[n] = jnp.broadcast_to(jnp.max(mpart_scr[n], axis=1, keepdims=True), mfull_scr.shape[1:])

    def far_acc(c):
        start = pl.multiple_of(c * chunk, chunk)
        sweep_acc(v_ref[0, pl.ds(start, chunk), :].astype(BF16), lgfar_scr.at[c])

    def far_acc_pair(i, carry):
        far_acc(2 * i)
        far_acc(2 * i + 1)
        return carry

    def far_acc_last(i, carry):
        far_acc(n_far - 1)
        return carry

    lax.fori_loop(0, n_far // 2, far_acc_pair, 0)
    lax.fori_loop(0, n_far % 2, far_acc_last, 0)
    sweep_acc(near_kv(v_ref), lgnear_scr)

    outs = [acc_scr[n] / jnp.sum(lpart_scr[n], axis=1, keepdims=True) for n in range(N_KV)]
    o_ref[...] = _unstack_heads(outs, rows).astype(o_ref.dtype)


def _dsa_prompt(z, bias_near):
    b, t, _ = z.shape
    nb = t // KEY_BLK
    assert nb % FAR_BLKS == 0
    topk = min(TOPK_MAX, t // 4)
    grows = GROUP * KEY_BLK
    chunk = FAR_BLKS * KEY_BLK
    qrows = Q_PAIR * KEY_BLK
    return pl.pallas_call(
        functools.partial(_dsa_prompt_kernel, topk=topk),
        grid=(b, nb // Q_PAIR),
        in_specs=[pl.BlockSpec((1, qrows, D_ATT), lambda bi, qi: (bi, qi, C_Q // D_ATT)),
                  pl.BlockSpec((1, qrows, 256), lambda bi, qi: (bi, qi, C_QI // 256)),
                  pl.BlockSpec((1, qrows, LANES), lambda bi, qi: (bi, qi, C_WI // LANES)),
                  pl.BlockSpec((1, t, 256), lambda bi, qi: (bi, 0, C_KI4 // 256)),
                  pl.BlockSpec((1, t, LANES), lambda bi, qi: (bi, 0, C_K // LANES)),
                  pl.BlockSpec((1, t, LANES), lambda bi, qi: (bi, 0, C_VV // LANES)),
                  pl.BlockSpec((N_KV, grows, 2 * KEY_BLK), lambda bi, qi: (0, 0, 0))],
        out_specs=pl.BlockSpec((1, qrows, D_ATT), lambda bi, qi: (bi, qi, 0)),
        out_shape=jax.ShapeDtypeStruct((b, t, D_ATT), BF16),
        scratch_shapes=[pltpu.VMEM((Q_PAIR, nb, KEY_BLK, KEY_BLK), I32),
                        pltpu.VMEM((N_KV, grows, LANES), BF16),
                        pltpu.VMEM((N_KV, grows, LANES), F32),
                        pltpu.VMEM((N_KV, grows, LANES), F32),
                        pltpu.VMEM((N_KV, grows, LANES), F32),
                        pltpu.VMEM((N_KV, grows, LANES), F32),
                        pltpu.VMEM((grows, chunk), BF16),
                        pltpu.VMEM((nb // FAR_BLKS, N_KV, grows, chunk), F32),
                        pltpu.VMEM((N_KV, grows, 2 * KEY_BLK), F32)],
        compiler_params=_cp(2),
        name="dsa_prompt",
    )(z, z, z, z, z, z, bias_near)


def _dsa_sample_score_kernel(pt_ref, qi_ref, wi_ref, kin_ref, *rest):
    n_pages = len(rest) - 1
    pages, o_ref = rest[:n_pages], rest[n_pages]
    t = qi_ref.shape[1]
    qi = qi_ref[0]
    wi = wi_ref[0]
    qs = jnp.concatenate([qi[:, h * D_IDX:(h + 1) * D_IDX] for h in range(N_IDX_HEADS)], axis=0).astype(BF16)
    k_past_t = jnp.concatenate([pg[0, 0] for pg in pages], axis=1).astype(BF16)
    k_new = jnp.concatenate([kin_ref[0][:, :D_IDX], jnp.zeros((KEY_BLK - t, D_IDX), F32)], axis=0).astype(BF16)

    def score(d):
        s = jnp.zeros((t, d.shape[1]), F32)
        for h in range(N_IDX_HEADS):
            s = s + wi[:, h:h + 1] * jnp.maximum(d[h * t:(h + 1) * t], 0.0)
        return s

    keys_past = _score_keys(score(_dot(qs, k_past_t)))
    for j in range(n_pages):
        o_ref[j] = keys_past[:, j * KEY_BLK:(j + 1) * KEY_BLK]
    s_new = score(_dot_nt(qs, k_new))
    lane = lax.broadcasted_iota(I32, s_new.shape, 1)
    trow = lax.broadcasted_iota(I32, s_new.shape, 0)
    o_ref[n_pages] = _score_keys(jnp.where(lane <= trow, s_new, -jnp.inf))


def _page_specs(n_pages, layer, feat):
    return [pl.BlockSpec((1, 1, feat, PAGE_SIZE), lambda bi, pt, j=j: (layer, pt[bi, j], 0, 0))
            for j in range(n_pages)]


def _dsa_sample_scores(z, cache_kidx, page_table, layer):
    b, t, _ = z.shape
    n_pages = page_table.shape[1]
    nblk = n_pages + 1
    grid_spec = pltpu.PrefetchScalarGridSpec(
        num_scalar_prefetch=1,
        grid=(b,),
        in_specs=[pl.BlockSpec((1, t, 256), lambda bi, pt: (bi, 0, C_QI // 256)),
                  pl.BlockSpec((1, t, LANES), lambda bi, pt: (bi, 0, C_WI // LANES)),
                  pl.BlockSpec((1, t, 256), lambda bi, pt: (bi, 0, C_KI4 // 256))]
        + _page_specs(n_pages, layer, D_IDX),
        out_specs=pl.BlockSpec((nblk, t, KEY_BLK), lambda bi, pt: (0, bi, 0)),
    )
    return pl.pallas_call(
        _dsa_sample_score_kernel,
        grid_spec=grid_spec,
        out_shape=jax.ShapeDtypeStruct((nblk, b * t, KEY_BLK), I32),
        compiler_params=_cp(1),
        name="dsa_sample_scores",
    )(page_table, z, z, z, *([cache_kidx] * n_pages))


def _dsa_sample_topk_kernel(key_ref, thr_ref, cut_ref, cut_scr, *, topk):
    nblk, rows, _ = key_ref.shape
    thr, cut = _topk_threshold(lambda p, j: key_ref[j], 1, nblk, rows, topk, 13, cut_scr)
    thr_ref[...] = thr
    cut_ref[...] = cut


def _dsa_sample_topk(keys, topk, rows=128):
    nblk, m, _ = keys.shape
    return pl.pallas_call(
        functools.partial(_dsa_sample_topk_kernel, topk=topk),
        grid=(m // rows,),
        in_specs=[pl.BlockSpec((nblk, rows, KEY_BLK), lambda i: (0, i, 0))],
        out_specs=[pl.BlockSpec((rows, 1), lambda i: (i, 0)),
                   pl.BlockSpec((rows, 1), lambda i: (i, 0))],
        out_shape=[jax.ShapeDtypeStruct((m, 1), I32), jax.ShapeDtypeStruct((m, 1), I32)],
        scratch_shapes=[pltpu.VMEM((rows, 1), I32)],
        compiler_params=_cp(1),
        name="dsa_sample_topk",
    )(keys)


def _dsa_sample_attn_kernel(pt_ref, q_ref, kn_ref, vn_ref, key_ref, thr_ref, cut_ref, bias_ref, *rest):
    n_pages = (len(rest) - 1) // 2
    kpages, vpages, o_ref = rest[:n_pages], rest[n_pages:2 * n_pages], rest[2 * n_pages]
    t = q_ref.shape[1]
    past = n_pages * PAGE_SIZE
    thr = thr_ref[...]
    cut = cut_ref[...]
    q = q_ref[0] * (HEAD_DIM ** -0.5 * LOG2E)
    trow = lax.broadcasted_iota(I32, (t, 1), 0)

    n_far = n_pages - 1
    kt_far = jnp.concatenate([pg[0, 0] for pg in kpages[:n_far]], axis=1).astype(BF16)
    vt_far = jnp.concatenate([pg[0, 0] for pg in vpages[:n_far]], axis=1).astype(BF16)
    keys_far = jnp.concatenate([key_ref[j] for j in range(n_far)], axis=1)
    g_far = lax.broadcasted_iota(I32, keys_far.shape, 1)
    neg_far = jnp.where(_select_mask(keys_far, thr, cut, g_far), 0.0, NEG_BIG)

    pad = jnp.zeros((KEY_BLK - t, LANES), F32)
    kt_last = kpages[n_far][0, 0].astype(BF16)
    vt_last = vpages[n_far][0, 0].astype(BF16)
    k_new = jnp.concatenate([kn_ref[0], pad], axis=0).astype(BF16)
    v_new = jnp.concatenate([vn_ref[0], pad], axis=0).astype(BF16)
    lane_b = lax.broadcasted_iota(I32, (t, KEY_BLK), 1)
    sel_last = _select_mask(key_ref[n_far], thr, cut, n_far * KEY_BLK + lane_b)
    sel_new = _select_mask(key_ref[n_pages], thr, cut, past + lane_b) & (lane_b <= trow)
    neg_near = jnp.concatenate([jnp.where(sel_last, 0.0, NEG_BIG), jnp.where(sel_new, 0.0, NEG_BIG)], axis=1)

    qs = jnp.concatenate([_stack_heads(q, n, t) for n in range(N_KV)], axis=0).astype(BF16)
    bias = jnp.concatenate([bias_ref[n] for n in range(N_KV)], axis=0)
    lg_far = _dot(qs, kt_far) + jnp.concatenate([neg_far] * N_HEADS, axis=0)
    lg_near = (jnp.concatenate([_dot(qs, kt_last), _dot_nt(qs, k_new)], axis=1) + bias
               + jnp.concatenate([neg_near] * N_HEADS, axis=0))
    m = jnp.maximum(jnp.max(lg_far, axis=1, keepdims=True), jnp.max(lg_near, axis=1, keepdims=True))
    p_far = jnp.exp2(lg_far - m)
    p_near = jnp.exp2(lg_near - m)
    l = jnp.sum(p_far, axis=1, keepdims=True) + jnp.sum(p_near, axis=1, keepdims=True)
    p_near = p_near.astype(BF16)
    o = (_dot_nt(p_far.astype(BF16), vt_far) + _dot_nt(p_near[:, :KEY_BLK], vt_last)
         + _dot(p_near[:, KEY_BLK:], v_new)) / l
    o_ref[0] = _unstack_heads([o[n * GROUP * t:(n + 1) * GROUP * t] for n in range(N_KV)], t)


def _dsa_sample_attn(z, keys, thr, cut, cache_k, cache_v, page_table, layer, bias_s):
    b, t, _ = z.shape
    n_pages = page_table.shape[1]
    nblk = n_pages + 1
    grid_spec = pltpu.PrefetchScalarGridSpec(
        num_scalar_prefetch=1,
        grid=(b,),
        in_specs=[pl.BlockSpec((1, t, D_ATT), lambda bi, pt: (bi, 0, C_Q // D_ATT)),
                  pl.BlockSpec((1, t, LANES), lambda bi, pt: (bi, 0, C_K // LANES)),
                  pl.BlockSpec((1, t, LANES), lambda bi, pt: (bi, 0, C_VV // LANES)),
                  pl.BlockSpec((nblk, t, KEY_BLK), lambda bi, pt: (0, bi, 0)),
                  pl.BlockSpec((t, 1), lambda bi, pt: (bi, 0)),
                  pl.BlockSpec((t, 1), lambda bi, pt: (bi, 0)),
                  pl.BlockSpec((N_KV, GROUP * t, 2 * KEY_BLK), lambda bi, pt: (0, 0, 0))]
        + _page_specs(n_pages, layer, LANES) + _page_specs(n_pages, layer, LANES),
        out_specs=pl.BlockSpec((1, t, D_ATT), lambda bi, pt: (bi, 0, 0)),
    )
    return pl.pallas_call(
        _dsa_sample_attn_kernel,
        grid_spec=grid_spec,
        out_shape=jax.ShapeDtypeStruct((b, t, D_ATT), F32),
        compiler_params=_cp(1),
        name="dsa_sample_attn",
    )(page_table, z, z, z, keys, thr, cut, bias_s, *([cache_k] * n_pages), *([cache_v] * n_pages))


def _merge_kernel(ya_ref, yb_ref, yc_ref, ga_ref, gb_ref, gc_ref, x_ref, g1_ref, sh2_ref, sc2_ref,
                  wpa_ref, wpb_ref, wpc_ref, wo_ref, lng_ref, lnb_ref, wrh_ref, wrl_ref, rb_ref,
                  x1_ref, h2_ref, gate_ref, *, alpha):
    gbk, rbk, d = x_ref.shape
    m = gbk * rbk

    def flat(ref):
        return ref[...].reshape(m, ref.shape[-1])

    merged = (_sigmoid(flat(ga_ref)) * _dot(flat(ya_ref).astype(BF16), wpa_ref[...])
              + _sigmoid(flat(gb_ref)) * _dot(flat(yb_ref).astype(BF16), wpb_ref[...])
              + _sigmoid(flat(gc_ref)) * _dot(flat(yc_ref).astype(BF16), wpc_ref[...]))
    mix = _dot(merged.astype(BF16), wo_ref[...]).reshape(gbk, rbk, d)
    x1 = _ln(alpha * x_ref[...] + (1.0 + g1_ref[...]) * mix, lng_ref[...], lnb_ref[...])
    x1_ref[...] = x1
    h2 = x1 * (1.0 + sc2_ref[...]) + sh2_ref[...]
    h2_ref[...] = h2.astype(h2_ref.dtype)

    h2f = h2.reshape(m, d)
    hi = h2f.astype(BF16)
    lo = (h2f - hi.astype(F32)).astype(BF16)
    logits = _dot_nt(wrh_ref[...], hi) + _dot_nt(wrl_ref[...], hi) + _dot_nt(wrh_ref[...], lo)
    scores = _sigmoid(logits)
    sel = scores + rb_ref[...]
    row = lax.broadcasted_iota(I32, (N_EXPERTS, m), 0).astype(F32)
    best = jnp.zeros((1, m), F32)
    best_score = None
    for g in range(N_GROUPS):
        r = [sel[g * EXPERTS_PER_GROUP + i:g * EXPERTS_PER_GROUP + i + 1] for i in range(EXPERTS_PER_GROUP)]
        top2 = None
        for i in range(EXPERTS_PER_GROUP):
            for j in range(i + 1, EXPERTS_PER_GROUP):
                pair = r[i] + r[j]
                top2 = pair if top2 is None else jnp.maximum(top2, pair)
        if best_score is None:
            best_score = top2
        else:
            better = top2 > best_score
            best = jnp.where(better, float(g), best)
            best_score = jnp.where(better, top2, best_score)
    lo_row = best * EXPERTS_PER_GROUP
    in_group = (row >= lo_row) & (row < lo_row + EXPERTS_PER_GROUP)
    masked = jnp.where(in_group, sel, -jnp.inf)
    m1 = jnp.max(masked, axis=0, keepdims=True)
    i1 = jnp.min(jnp.where(masked == m1, row, float(N_EXPERTS)), axis=0, keepdims=True)
    masked2 = jnp.where(row == i1, -jnp.inf, masked)
    m2 = jnp.max(masked2, axis=0, keepdims=True)
    i2 = jnp.min(jnp.where(masked2 == m2, row, float(N_EXPERTS)), axis=0, keepdims=True)
    w1 = jnp.sum(jnp.where(row == i1, scores, 0.0), axis=0, keepdims=True)
    w2 = jnp.sum(jnp.where(row == i2, scores, 0.0), axis=0, keepdims=True)
    tot = w1 + w2
    gate_ref[...] = jnp.where(row == i1, w1 / tot, 0.0) + jnp.where(row == i2, w2 / tot, 0.0)


def _merge(ya, yb, yc, z, x3, mod, w_pa, w_pb, w_pc, w_o, lng, lnb, wr_hi, wr_lo, rbias, gb, rb, act_dtype, alpha):
    g, r, d = x3.shape
    rt = r // rb

    def act(width, col):
        return pl.BlockSpec((gb, rb, width), lambda gi, ri: (gi, ri, col))

    def modspec(col):
        return pl.BlockSpec((gb, 1, d), lambda gi, ri: (gi, 0, col))

    def full(a):
        return pl.BlockSpec(a.shape, lambda gi, ri: (0,) * a.ndim)

    return pl.pallas_call(
        functools.partial(_merge_kernel, alpha=alpha),
        grid=(g // gb, rt),
        in_specs=[act(D_A, 0), act(D_ATT, 0), act(D_C, 0),
                  act(d, C_GA // d), act(d, C_GB // d), act(d, C_GC // d),
                  act(d, 0), modspec(2), modspec(3), modspec(4),
                  full(w_pa), full(w_pb), full(w_pc), full(w_o), full(lng), full(lnb),
                  full(wr_hi), full(wr_lo), full(rbias)],
        out_specs=[act(d, 0), act(d, 0),
                   pl.BlockSpec((N_EXPERTS, gb * rb), lambda gi, ri: (0, gi * rt + ri))],
        out_shape=[jax.ShapeDtypeStruct((g, r, d), F32),
                   jax.ShapeDtypeStruct((g, r, d), act_dtype),
                   jax.ShapeDtypeStruct((N_EXPERTS, g * r), F32)],
        compiler_params=_cp(2),
        name="merge",
    )(ya, yb, yc, z, z, z, x3, mod, mod, mod, w_pa, w_pb, w_pc, w_o, lng, lnb, wr_hi, wr_lo, rbias)


MOE_EXPERTS = 2


def _moe_kernel(h_ref, gate_ref, x1_ref, g2_ref, wg_ref, wu_ref, wd_ref, lng_ref, lnb_ref, o_ref, acc_scr, *, alpha):
    e = pl.program_id(2)
    gbk, rbk, d = h_ref.shape
    m = gbk * rbk

    @pl.when(e == 0)
    def _():
        acc_scr[...] = jnp.zeros(acc_scr.shape, F32)

    h = h_ref[...].reshape(m, d).astype(BF16)
    gate = gate_ref[...]
    lane = lax.broadcasted_iota(I32, gate.shape, 1)
    hes = []
    for j in range(MOE_EXPERTS):
        a = _dot(h, wg_ref[0, j])
        u = _dot(h, wu_ref[0, j])
        gcol = jnp.sum(jnp.where(lane == e * MOE_EXPERTS + j, gate, 0.0), axis=1, keepdims=True)
        hes.append(((a * _sigmoid(a)) * u * gcol).astype(BF16))
    wd = wd_ref[0].reshape(MOE_EXPERTS * D_EXPERT, d)
    acc_scr[...] += _dot(jnp.concatenate(hes, axis=1), wd)

    @pl.when(e == pl.num_programs(2) - 1)
    def _():
        y = acc_scr[...].reshape(gbk, rbk, d)
        o_ref[...] = _ln(alpha * x1_ref[...] + (1.0 + g2_ref[...]) * y, lng_ref[...], lnb_ref[...])


def _moe(h2, gate, x1, mod, w_g, w_u, w_d, lng, lnb, layer, gb, rb, alpha):
    g, r, d = x1.shape
    rt = r // rb
    n_e = w_g.shape[1]
    return pl.pallas_call(
        functools.partial(_moe_kernel, alpha=alpha),
        grid=(g // gb, rt, n_e // MOE_EXPERTS),
        in_specs=[pl.BlockSpec((gb, rb, d), lambda gi, ri, e: (gi, ri, 0)),
                  pl.BlockSpec((gb * rb, n_e), lambda gi, ri, e: (gi * rt + ri, 0)),
                  pl.BlockSpec((gb, rb, d), lambda gi, ri, e: (gi, ri, 0)),
                  pl.BlockSpec((gb, 1, d), lambda gi, ri, e: (gi, 0, 5)),
                  pl.BlockSpec((1, MOE_EXPERTS, d, D_EXPERT), lambda gi, ri, e: (layer, e, 0, 0)),
                  pl.BlockSpec((1, MOE_EXPERTS, d, D_EXPERT), lambda gi, ri, e: (layer, e, 0, 0)),
                  pl.BlockSpec((1, MOE_EXPERTS, D_EXPERT, d), lambda gi, ri, e: (layer, e, 0, 0)),
                  pl.BlockSpec((1, d), lambda gi, ri, e: (0, 0)),
                  pl.BlockSpec((1, d), lambda gi, ri, e: (0, 0))],
        out_specs=pl.BlockSpec((gb, rb, d), lambda gi, ri, e: (gi, ri, 0)),
        out_shape=jax.ShapeDtypeStruct((g, r, d), F32),
        scratch_shapes=[pltpu.VMEM((gb * rb, d), F32)],
        compiler_params=_cp(3),
        name="moe",
    )(h2, gate, x1, mod, w_g, w_u, w_d, lng, lnb)


def _pad_cols(n, dtype, like):
    return jnp.zeros(like.shape[:-1] + (n,), dtype)


def _relayout_in(w):
    def sl(o, n):
        return w[..., o:o + n]
    ki = sl(_O_KI, D_IDX)
    parts = [sl(_O_AGLU, 2 * D_A), sl(_O_UV, 2 * D_C), sl(_O_G, 3 * D_MODEL), sl(_O_Q, D_ATT),
             sl(_O_QI, N_IDX_HEADS * D_IDX), sl(_O_K, N_KV * HEAD_DIM), sl(_O_V, N_KV * HEAD_DIM),
             ki, ki, ki, ki, sl(_O_WI, N_IDX_HEADS), _pad_cols(D_ZP - C_WI - N_IDX_HEADS, w.dtype, w)]
    return jnp.concatenate(parts, axis=-1)


def kernel(x_prompt, x_sample, cache_k, cache_v, cache_kidx, state_conv, page_table, c_prompt, c_sample,
           w_ada, b_ada, w_in, b_in, conv_k, conv_b, lnA_g, lnA_b, lnC_g, lnC_b, w_s, b_s,
           w_pa, w_pb, w_pc, w_o, ln1_g, ln1_b, ln2_g, ln2_b, rel_table, w_router, router_bias,
           w_e_gate, w_e_up, w_e_down):
    depth = w_in.shape[0]
    alpha = (2 * depth) ** 0.25
    bp, tp, d = x_prompt.shape
    bs, ts, _ = x_sample.shape
    n_pages = page_table.shape[1]
    past = n_pages * PAGE_SIZE
    topk_s = min(TOPK_MAX, (past + ts) // 4)
    n_pool = cache_k.shape[1]

    w_in_p = _relayout_in(w_in).astype(BF16)
    b_in_p = _relayout_in(b_in).reshape(depth, 1, D_ZP)
    w_pa_b, w_pb_b, w_pc_b, w_o_b = (w.astype(BF16) for w in (w_pa, w_pb, w_pc, w_o))
    w_g_b, w_u_b, w_d_b = (w.astype(BF16) for w in (w_e_gate, w_e_up, w_e_down))
    wr_t = w_router.T
    wr_hi = wr_t.astype(BF16)
    wr_lo = (wr_t - wr_hi.astype(F32)).astype(BF16)
    rbias = router_bias.reshape(N_EXPERTS, 1)
    grp_w = D_C // N_CGROUPS
    bs_tab_p = jnp.repeat(jnp.swapaxes(b_s[:, :, :CHUNK], 1, 2), grp_w, axis=2)
    bs_tab_s = bs_tab_p[:, :ts]
    wl_tab = jnp.repeat(jnp.transpose(w_s[:, :, :ts, :ts], (0, 3, 2, 1)), grp_w, axis=3)
    cache_k2 = jnp.transpose(cache_k, (0, 1, 3, 4, 2)).reshape(depth, n_pool, N_KV * HEAD_DIM, PAGE_SIZE)
    cache_v2 = jnp.transpose(cache_v, (0, 1, 3, 4, 2)).reshape(depth, n_pool, N_KV * HEAD_DIM, PAGE_SIZE)
    cache_ki2 = jnp.transpose(cache_kidx, (0, 1, 3, 2))

    def row(a, l):
        return a[l].reshape(1, -1)

    bias_p = _bias_near(rel_table)
    bias_s = bias_p.reshape(N_KV, GROUP, KEY_BLK, 2 * KEY_BLK)[:, :, :ts].reshape(N_KV, GROUP * ts, 2 * KEY_BLK)

    n_c = bp + bs
    c_all = jnp.concatenate([c_prompt, c_sample, jnp.zeros((-n_c % 8, d), F32)], axis=0)
    mod_all = _ada(c_all, w_ada, b_ada)

    yp, ys = x_prompt, x_sample
    outs = {k: [] for k in ("kp", "vp", "kip", "cp", "ks", "vs", "kis", "cs", "vcs")}
    rb_p = 512
    for l in range(depth):
        mod_p = mod_all[l, :bp].reshape(bp, 1, 6 * d)
        mod_s = mod_all[l, bp:n_c].reshape(bs, 1, 6 * d)
        lnA = (row(lnA_g, l), row(lnA_b, l))
        lnC = (row(lnC_g, l), row(lnC_b, l))
        ln1 = (row(ln1_g, l), row(ln1_b, l))
        ln2 = (row(ln2_g, l), row(ln2_b, l))

        z = _inproj(yp, mod_p, w_in_p[l], b_in_p[l], 1, rb_p, 3328)
        ya, conv_state = _conv_prompt(z, conv_k[l], row(conv_b, l), *lnA)
        yb = _dsa_prompt(z, bias_p)
        yc = _sgu_prompt(z, *lnC, w_s[l], bs_tab_p[l])
        x1, h2, gate_t = _merge(ya, yb, yc, z, yp, mod_p, w_pa_b[l], w_pb_b[l], w_pc_b[l], w_o_b[l], *ln1,
                                wr_hi, wr_lo, rbias, 1, rb_p, BF16, alpha)
        yp = _moe(h2, gate_t.T, x1, mod_p, w_g_b, w_u_b, w_d_b, *ln2, l, 1, 1024, alpha)
        outs["kp"].append(z[:, :, C_K:C_K + N_KV * HEAD_DIM].reshape(bp, tp, N_KV, HEAD_DIM))
        outs["vp"].append(z[:, :, C_VV:C_VV + N_KV * HEAD_DIM].reshape(bp, tp, N_KV, HEAD_DIM))
        outs["kip"].append(z[:, :, C_KI4:C_KI4 + D_IDX])
        outs["cp"].append(conv_state)

        zs = _inproj(ys, mod_s, w_in_p[l], b_in_p[l], bs, ts, 1664)
        ya, conv_state = _conv_sample(zs, state_conv[l], conv_k[l], row(conv_b, l), *lnA)
        keys = _dsa_sample_scores(zs, cache_ki2, page_table, l)
        thr, cut = _dsa_sample_topk(keys, topk_s)
        yb = _dsa_sample_attn(zs, keys, thr, cut, cache_k2, cache_v2, page_table, l, bias_s)
        yc, v_chunk = _sgu_sample(zs, *lnC, wl_tab[l], bs_tab_s[l])
        x1, h2, gate_t = _merge(ya, yb, yc, zs, ys, mod_s, w_pa_b[l], w_pb_b[l], w_pc_b[l], w_o_b[l], *ln1,
                                wr_hi, wr_lo, rbias, bs, ts, F32, alpha)
        ys = _moe(h2, gate_t.T, x1, mod_s, w_g_b, w_u_b, w_d_b, *ln2, l, bs, ts, alpha)
        outs["ks"].append(zs[:, :, C_K:C_K + N_KV * HEAD_DIM].reshape(bs, ts, N_KV, HEAD_DIM))
        outs["vs"].append(zs[:, :, C_VV:C_VV + N_KV * HEAD_DIM].reshape(bs, ts, N_KV, HEAD_DIM))
        outs["kis"].append(zs[:, :, C_KI4:C_KI4 + D_IDX])
        outs["cs"].append(conv_state)
        outs["vcs"].append(v_chunk)

    st = {k: jnp.stack(v) for k, v in outs.items()}
    return (yp, ys, st["kp"], st["vp"], st["kip"], st["cp"],
            st["ks"], st["vs"], st["kis"], st["cs"], st["vcs"])
```
